```python
import jax, jax.numpy as jnp
from jax import lax
import numpy as np

D_MODEL = 4096
BATCH = 1
SEQ = 16384
DEPTH = 2

GRID_W = 64
CTX_LEN = 256
QB = 128
ROPE_THETA = 10000.0
NORM_EPS = 1e-6
H_A = 16
MLA_NOPE = 128
MLA_ROPE = 64
MLA_V = 128
Q_LORA = 1024
KV_LORA = 512
MLA_SCALE = (MLA_NOPE + MLA_ROPE) ** -0.5
H_B = 32
N_B = 64
W_B = H_B * N_B
R_W = 96
R_A = 96
R_G = 256
GN_EPS = 64e-5
H_C = 32
KV_C = 4
HD_C = 64
WINDOW = 128
W_C = H_C * HD_C
SWA_SCALE = HD_C ** -0.5
N_BRANCH = 3
W_BR = 2048
N_EXPERTS = 32
TOP_K = 4
D_EXPERT = 512
SWIGLU_LIMIT = 7.0
SWIGLU_ALPHA = 1.702
MOE_BLOCK = 128
C_MLA = Q_LORA + KV_LORA + MLA_ROPE
C_RWKV = 3 * W_B + 2 * R_W + 2 * R_A + R_G
C_SWA = W_C + 2 * KV_C * HD_C
C_GATE = N_BRANCH * D_MODEL
C_IN = C_MLA + C_RWKV + C_SWA + C_GATE

kernel_name = 'hybrid_mla_rwkv7_swa_moe_diffusion_trunk'


def split_cols(t, sizes):
    idx = [int(s) for s in np.cumsum(sizes)[:-1]]
    return jnp.split(t, idx, axis=-1)


def rmsnorm(x, g, eps=NORM_EPS):
    xf = x.astype(jnp.float32)
    y = xf * lax.rsqrt(jnp.mean(xf * xf, -1, keepdims=True) + eps)
    return (y * g.astype(jnp.float32)).astype(x.dtype)


def modulate(t, g, shift, scale):
    return rmsnorm(t, g) * (1 + scale) + shift


def axial_rope_tables(rows, cols, dim):
    half = dim // 2
    inv = ROPE_THETA ** (-jnp.arange(0, half, 2, dtype=jnp.float32) / half)
    ar = rows.astype(jnp.float32)[:, None] * inv
    ac = cols.astype(jnp.float32)[:, None] * inv
    ang = jnp.concatenate([ar, ar, ac, ac], -1)
    return jnp.cos(ang), jnp.sin(ang)


def apply_rope2d(x, cos, sin):
    half = x.shape[-1] // 2
    qtr = half // 2

    def rot(p):
        return jnp.concatenate([-p[..., qtr:], p[..., :qtr]], -1)

    xr = jnp.concatenate([rot(x[..., :half]), rot(x[..., half:])], -1)
    return x * cos[:, None].astype(x.dtype) + xr * sin[:, None].astype(x.dtype)


def block_attention(q, k, v, scale):
    B, T, H, dq = q.shape
    nb = T // QB
    qb = q.reshape(B, nb, QB, H, dq).swapaxes(0, 1)

    def one(qi):
        s = jnp.einsum('bqhd,bkhd->bhqk', qi, k).astype(jnp.float32) * scale
        p = jax.nn.softmax(s, -1).astype(v.dtype)
        return jnp.einsum('bhqk,bkhd->bqhd', p, v)

    o = lax.map(one, qb)
    return o.swapaxes(0, 1).reshape(B, T, H * v.shape[-1])


def sink_softmax(s, sink):
    sk = jnp.broadcast_to(sink.astype(jnp.float32).reshape(1, KV_C, -1, 1, 1), s.shape[:-1] + (1,))
    p = jax.nn.softmax(jnp.concatenate([s, sk], -1), -1)
    return p[..., :-1]


def ctx_attention(q, k, v, sink):
    B, L = q.shape[:2]
    s = jnp.einsum('bqkgd,bskd->bkgqs', q, k).astype(jnp.float32) * SWA_SCALE
    p = sink_softmax(s, sink).astype(v.dtype)
    return jnp.einsum('bkgqs,bskd->bqkgd', p, v).reshape(B, L, W_C)


def window_attention(q, k, v, kc, vc, sink):
    B, S = q.shape[:2]
    nb = S // QB
    L = kc.shape[1]

    def band(t):
        tp = jnp.pad(t, ((0, 0), (QB, QB), (0, 0), (0, 0))).reshape(B, nb + 2, QB, KV_C, HD_C)
        return jnp.concatenate([tp[:, :-2], tp[:, 1:-1], tp[:, 2:]], 2).swapaxes(0, 1)

    qb = q.reshape(B, nb, QB, KV_C, H_C // KV_C, HD_C).swapaxes(0, 1)
    q_off = jnp.arange(QB)
    k_off = jnp.arange(3 * QB) - QB

    def one(args):
        qi, ki, vi, i = args
        qpos = i * QB + q_off
        kpos = i * QB + k_off
        mask = (jnp.abs(qpos[:, None] - kpos[None, :]) <= WINDOW) & (kpos >= 0)[None, :] & (kpos < S)[None, :]
        s_lat = jnp.einsum('bqkgd,bskd->bkgqs', qi, ki).astype(jnp.float32) * SWA_SCALE
        s_lat = jnp.where(mask, s_lat, -jnp.inf)
        s_ctx = jnp.einsum('bqkgd,bskd->bkgqs', qi, kc).astype(jnp.float32) * SWA_SCALE
        p = sink_softmax(jnp.concatenate([s_ctx, s_lat], -1), sink).astype(vi.dtype)
        return (jnp.einsum('bkgqs,bskd->bqkgd', p[..., :L], vc)
                + jnp.einsum('bkgqs,bskd->bqkgd', p[..., L:], vi))

    o = lax.map(one, (qb, band(k), band(v), jnp.arange(nb)))
    return o.swapaxes(0, 1).reshape(B, S, W_C)


def gqa_heads(pq, pk, pv, rope):
    B, T, _ = pq.shape
    q = pq.reshape(B, T, H_C, HD_C)
    k = pk.reshape(B, T, KV_C, HD_C)
    v = pv.reshape(B, T, KV_C, HD_C)
    if rope is not None:
        q = apply_rope2d(q, *rope)
        k = apply_rope2d(k, *rope)
    return q.reshape(B, T, KV_C, H_C // KV_C, HD_C), k, v


def mla_q(pq, qn_g, wq_up, rope):
    B, T, _ = pq.shape
    q = (rmsnorm(pq, qn_g) @ wq_up).reshape(B, T, H_A, MLA_NOPE + MLA_ROPE)
    q_nope, q_rope = q[..., :MLA_NOPE], q[..., MLA_NOPE:]
    if rope is not None:
        q_rope = apply_rope2d(q_rope, *rope)
    return jnp.concatenate([q_nope, q_rope], -1)


def mla_kv(pkv, pkr, kvn_g, wkv_up, rope):
    B, T, _ = pkv.shape
    kv = (rmsnorm(pkv, kvn_g) @ wkv_up).reshape(B, T, H_A, MLA_NOPE + MLA_V)
    k_nope, v = kv[..., :MLA_NOPE], kv[..., MLA_NOPE:]
    k_rope = pkr[:, :, None, :]
    if rope is not None:
        k_rope = apply_rope2d(k_rope, *rope)
    k = jnp.concatenate([k_nope, jnp.broadcast_to(k_rope, (B, T, H_A, MLA_ROPE))], -1)
    return k, v


def centred_shift(p, mu):
    prev = jnp.pad(p, ((0, 0), (1, 0), (0, 0)))[:, :-1]
    nxt = jnp.pad(p, ((0, 0), (0, 1), (0, 0)))[:, 1:]
    return p + mu[0] * (prev - p) + mu[1] * (nxt - p)


def rwkv_prep(p, lp):
    B, T, _ = p.shape
    f32 = jnp.float32
    r, k, v, wd_f, wd_b, ad_f, ad_b, gd = split_cols(p, (W_B, W_B, W_B, R_W, R_W, R_A, R_A, R_G))

    def heads(t):
        return t.reshape(B, T, H_B, N_B)

    kk = heads((k * lp['rwkv_k_k']).astype(f32))
    kk = kk * lax.rsqrt(jnp.sum(kk * kk, -1, keepdims=True) + 1e-12)
    dirs = []
    for d, (wd, ad) in enumerate(((wd_f, ad_f), (wd_b, ad_b))):
        w_pre = (lp['rwkv_w0'][d] + jnp.tanh(wd) @ lp['rwkv_w_up'][d]).astype(f32)
        decay = jnp.exp(-jnp.exp(-jax.nn.softplus(-w_pre) - 0.5))
        a = jax.nn.sigmoid((lp['rwkv_a0'][d] + ad @ lp['rwkv_a_up'][d]).astype(f32))
        k_d = k.astype(f32) * (1.0 + (a - 1.0) * lp['rwkv_k_a'].astype(f32))
        dirs.append((heads(decay), heads(k_d), heads(a)))
    g = jax.nn.sigmoid(gd) @ lp['rwkv_g_up']
    return heads(r), heads(v), kk, dirs, g


def rwkv_scan(s0, r, w, k, v, kk, a, reverse):
    seq = tuple(jnp.moveaxis(t.astype(jnp.float32), 1, 0) for t in (r, w, k, v, kk, a))

    def step(s, inp):
        r_t, w_t, k_t, v_t, kk_t, a_t = inp
        sa = jnp.einsum('bhvk,bhk->bhv', s, kk_t)
        s = (s * w_t[:, :, None, :] - sa[..., None] * (kk_t * a_t)[:, :, None, :]
             + v_t[..., None] * k_t[:, :, None, :])
        return s, jnp.einsum('bhvk,bhk->bhv', s, r_t)

    s, out = lax.scan(step, s0, seq, reverse=reverse)
    return s, jnp.moveaxis(out, 0, 1)


def rwkv_out(o, r, v, dirs, g, lp, dt):
    B, T = o.shape[:2]
    mu = jnp.mean(o, -1, keepdims=True)
    var = jnp.mean(jnp.square(o - mu), -1, keepdims=True)
    on = ((o - mu) * lax.rsqrt(var + GN_EPS)).reshape(B, T, W_B) * lp['rwkv_ln_g'] + lp['rwkv_ln_b']
    r_k = lp['rwkv_r_k'].reshape(H_B, N_B).astype(jnp.float32)
    rf = r.astype(jnp.float32)
    coef = (jnp.sum(rf * dirs[0][1] * r_k, -1, keepdims=True)
            + jnp.sum(rf * dirs[1][1] * r_k, -1, keepdims=True))
    bonus = (coef * v).reshape(B, T, W_B)
    return ((on + bonus) * g).astype(dt)


def rwkv_mixer(pc, px, lp, need_ctx):
    B = px.shape[0]
    rc, vc, kkc, dc, gc = rwkv_prep(centred_shift(pc, lp['rwkv_mu']), lp)
    rx, vx, kkx, dx, gx = rwkv_prep(centred_shift(px, lp['rwkv_mu']), lp)
    s0 = jnp.zeros((B, H_B, N_B, N_B), jnp.float32)
    oc_dir, ox_dir = [], []
    for d in range(2):
        wc, kc, ac = dc[d]
        wx, kx, ax = dx[d]
        s_ctx, oc = rwkv_scan(s0, rc, wc, kc, vc, kkc, ac, d == 1)
        _, ox = rwkv_scan(s_ctx, rx, wx, kx, vx, kkx, ax, d == 1)
        oc_dir.append(oc)
        ox_dir.append(ox)
    out_x = rwkv_out(ox_dir[0] + ox_dir[1], rx, vx, dx, gx, lp, px.dtype)
    out_c = rwkv_out(oc_dir[0] + oc_dir[1], rc, vc, dc, gc, lp, pc.dtype) if need_ctx else None
    return out_x, out_c


def merge_branches(o_a, o_b, o_c, gate_pre, w_branch, w_out):
    g_a, g_b, g_c = jnp.split(jax.nn.sigmoid(gate_pre), N_BRANCH, axis=-1)
    m = g_a * (o_a @ w_branch[0]) + g_b * (o_b @ w_branch[1]) + g_c * (o_c @ w_branch[2])
    return m @ w_out


def moe_ffn(h, router_w, router_b, w_gu, b_gu, w_down, b_down):
    T, D = h.shape
    logits = (h @ router_w + router_b).astype(jnp.float32)
    top_v, top_e = lax.top_k(logits, TOP_K)
    gates = jax.nn.softmax(top_v, -1).astype(h.dtype)
    tk = T * TOP_K
    flat_e = top_e.reshape(tk)
    flat_t = jnp.repeat(jnp.arange(T, dtype=jnp.int32), TOP_K)
    flat_g = gates.reshape(tk)
    order = jnp.argsort(flat_e)
    se, st, sg = flat_e[order], flat_t[order], flat_g[order]
    counts = jnp.bincount(flat_e, length=N_EXPERTS)
    starts = jnp.cumsum(counts) - counts
    pcounts = (counts + MOE_BLOCK - 1) // MOE_BLOCK * MOE_BLOCK
    pends = jnp.cumsum(pcounts)
    pstarts = pends - pcounts
    dest = pstarts[se] + jnp.arange(tk, dtype=jnp.int32) - starts[se]
    n_blocks = -(-tk // MOE_BLOCK) + N_EXPERTS
    n_rows = n_blocks * MOE_BLOCK
    row_tok = jnp.zeros((n_rows,), jnp.int32).at[dest].set(st)
    row_gate = jnp.zeros((n_rows,), h.dtype).at[dest].set(sg)
    block_e = jnp.minimum(
        jnp.searchsorted(pends, jnp.arange(n_blocks, dtype=pends.dtype) * MOE_BLOCK, side='right'),
        N_EXPERTS - 1)

    def expert_block(args):
        toks, g, e = args
        gu = h[toks] @ w_gu[e] + b_gu[e]
        gate = jnp.minimum(gu[:, :D_EXPERT], SWIGLU_LIMIT)
        up = jnp.clip(gu[:, D_EXPERT:], -SWIGLU_LIMIT, SWIGLU_LIMIT)
        act = (up + 1.0) * gate * jax.nn.sigmoid(gate * SWIGLU_ALPHA)
        return (act @ w_down[e] + b_down[e]) * g[:, None]

    yb = lax.map(expert_block, (row_tok.reshape(n_blocks, MOE_BLOCK),
                                row_gate.reshape(n_blocks, MOE_BLOCK), block_e))
    return jax.ops.segment_sum(yb.reshape(n_rows, D), row_tok, num_segments=T)


def trunk_layer(x, xc, c, c_ctx, rope, lp, need_ctx):
    B, S, D = x.shape
    L = xc.shape[1]
    mod_x = [m[:, None, :] for m in jnp.split(jax.nn.silu(c) @ lp['ada_w'] + lp['ada_b'], 6, axis=-1)]
    mod_c = jnp.split(jax.nn.silu(c_ctx) @ lp['ada_w'] + lp['ada_b'], 6, axis=-1)
    hx = modulate(x, lp['norm1_g'], mod_x[0], mod_x[1])
    hc = modulate(xc, lp['norm1_g'], mod_c[0], mod_c[1])
    sizes = (Q_LORA, KV_LORA, MLA_ROPE, C_RWKV, W_C, KV_C * HD_C, KV_C * HD_C, C_GATE)
    xq_a, xkv_a, xkr_a, x_rw, xq_c, xk_c, xv_c, x_gate = split_cols(hx @ lp['w_in'], sizes)
    cq_a, ckv_a, ckr_a, c_rw, cq_c, ck_c, cv_c, c_gate = split_cols(hc @ lp['w_in'], sizes)

    kc_a, vc_a = mla_kv(ckv_a, ckr_a, lp['mla_kvn_g'], lp['mla_wkv_up'], None)
    kx_a, vx_a = mla_kv(xkv_a, xkr_a, lp['mla_kvn_g'], lp['mla_wkv_up'], rope)
    qx_a = mla_q(xq_a, lp['mla_qn_g'], lp['mla_wq_up'], rope)
    ox_a = block_attention(qx_a, jnp.concatenate([kc_a, kx_a], 1), jnp.concatenate([vc_a, vx_a], 1), MLA_SCALE)
    ox_b, oc_b = rwkv_mixer(c_rw, x_rw, lp, need_ctx)
    qx_c, kx_c, vx_c = gqa_heads(xq_c, xk_c, xv_c, rope)
    qc_c, kc_c, vc_c = gqa_heads(cq_c, ck_c, cv_c, None)
    ox_c = window_attention(qx_c, kx_c, vx_c, kc_c, vc_c, lp['swa_sink'])

    x = x + mod_x[2] * merge_branches(ox_a, ox_b, ox_c, x_gate, lp['w_branch'], lp['w_out'])
    h2x = modulate(x, lp['norm2_g'], mod_x[3], mod_x[4])
    moe_args = (lp['router_w'], lp['router_b'], lp['exp_w_gu'], lp['exp_b_gu'], lp['exp_w_down'], lp['exp_b_down'])
    if need_ctx:
        qc_a = mla_q(cq_a, lp['mla_qn_g'], lp['mla_wq_up'], None)
        oc_a = block_attention(qc_a, kc_a, vc_a, MLA_SCALE)
        oc_c = ctx_attention(qc_c, kc_c, vc_c, lp['swa_sink'])
        xc = xc + mod_c[2] * merge_branches(oc_a, oc_b, oc_c, c_gate, lp['w_branch'], lp['w_out'])
        h2c = modulate(xc, lp['norm2_g'], mod_c[3], mod_c[4])
        y = moe_ffn(jnp.concatenate([h2c.reshape(B * L, D), h2x.reshape(B * S, D)], 0), *moe_args)
        xc = xc + mod_c[5] * y[:B * L].reshape(B, L, D)
        yx = y[B * L:]
    else:
        yx = moe_ffn(h2x.reshape(B * S, D), *moe_args)
    x = x + mod_x[5] * yx.reshape(B, S, D)
    return x, xc


def setup_inputs(seed: int = 0) -> dict:
    key = jax.random.key(seed)
    ks = iter(jax.random.split(key, 40))
    D = D_MODEL
    L = DEPTH

    def nrm(shape, s):
        return jax.random.normal(next(ks), shape, jnp.float32) * s

    def uni(shape, lo, hi):
        return jax.random.uniform(next(ks), shape, jnp.float32, lo, hi)

    return {
        'x': nrm((BATCH, SEQ, D), 1.0),
        'c': nrm((BATCH, D), 1.0),
        'ctx': nrm((BATCH, CTX_LEN, D), 1.0),
        'c_ctx': nrm((D,), 1.0),
        'ada_w': nrm((L, D, 6 * D), 0.5 * D ** -0.5),
        'ada_b': nrm((L, 6 * D), 0.01),
        'norm1_g': 1.0 + nrm((L, D), 0.02),
        'w_in': nrm((L, D, C_IN), D ** -0.5),
        'mla_qn_g': 1.0 + nrm((L, Q_LORA), 0.02),
        'mla_kvn_g': 1.0 + nrm((L, KV_LORA), 0.02),
        'mla_wq_up': nrm((L, Q_LORA, H_A * (MLA_NOPE + MLA_ROPE)), Q_LORA ** -0.5),
        'mla_wkv_up': nrm((L, KV_LORA, H_A * (MLA_NOPE + MLA_V)), KV_LORA ** -0.5),
        'rwkv_mu': uni((L, 2, C_RWKV), 0.0, 0.5),
        'rwkv_w0': uni((L, 2, W_B), -6.0, -1.0),
        'rwkv_w_up': nrm((L, 2, R_W, W_B), 0.1 * R_W ** -0.5),
        'rwkv_a0': nrm((L, 2, W_B), 0.1),
        'rwkv_a_up': nrm((L, 2, R_A, W_B), 0.5 * R_A ** -0.5),
        'rwkv_g_up': nrm((L, R_G, W_B), R_G ** -0.5),
        'rwkv_k_k': 0.85 + nrm((L, W_B), 0.02),
        'rwkv_k_a': 1.0 + nrm((L, W_B), 0.02),
        'rwkv_r_k': nrm((L, W_B), 0.1),
        'rwkv_ln_g': 1.0 + nrm((L, W_B), 0.02),
        'rwkv_ln_b': nrm((L, W_B), 0.01),
        'swa_sink': nrm((L, H_C), 0.5),
        'w_branch': nrm((L, N_BRANCH, W_BR, D), W_BR ** -0.5),
        'w_out': nrm((L, D, D), D ** -0.5),
        'norm2_g': 1.0 + nrm((L, D), 0.02),
        'router_w': nrm((L, D, N_EXPERTS), D ** -0.5),
        'router_b': nrm((L, N_EXPERTS), 0.01),
        'exp_w_gu': nrm((L, N_EXPERTS, D, 2 * D_EXPERT), D ** -0.5),
        'exp_b_gu': nrm((L, N_EXPERTS, 2 * D_EXPERT), 0.01),
        'exp_w_down': nrm((L, N_EXPERTS, D_EXPERT, D), D_EXPERT ** -0.5),
        'exp_b_down': nrm((L, N_EXPERTS, D), 0.01),
        'final_g': 1.0 + nrm((D,), 0.02),
    }


def reference(x, c, ctx, c_ctx, ada_w, ada_b, norm1_g, w_in, mla_qn_g, mla_kvn_g, mla_wq_up, mla_wkv_up,
              rwkv_mu, rwkv_w0, rwkv_w_up, rwkv_a0, rwkv_a_up, rwkv_g_up, rwkv_k_k, rwkv_k_a, rwkv_r_k,
              rwkv_ln_g, rwkv_ln_b, swa_sink, w_branch, w_out, norm2_g, router_w, router_b,
              exp_w_gu, exp_b_gu, exp_w_down, exp_b_down, final_g):
    S = x.shape[1]
    rows_n = S // GRID_W
    rows = jnp.repeat(jnp.arange(rows_n), GRID_W)
    cols = jnp.tile(jnp.arange(GRID_W), rows_n)
    rope = axial_rope_tables(rows, cols, MLA_ROPE)
    xc = ctx
    for i in range(DEPTH):
        lp = dict(ada_w=ada_w[i], ada_b=ada_b[i], norm1_g=norm1_g[i], w_in=w_in[i],
                  mla_qn_g=mla_qn_g[i], mla_kvn_g=mla_kvn_g[i], mla_wq_up=mla_wq_up[i], mla_wkv_up=mla_wkv_up[i],
                  rwkv_mu=rwkv_mu[i], rwkv_w0=rwkv_w0[i], rwkv_w_up=rwkv_w_up[i], rwkv_a0=rwkv_a0[i],
                  rwkv_a_up=rwkv_a_up[i], rwkv_g_up=rwkv_g_up[i], rwkv_k_k=rwkv_k_k[i], rwkv_k_a=rwkv_k_a[i],
                  rwkv_r_k=rwkv_r_k[i], rwkv_ln_g=rwkv_ln_g[i], rwkv_ln_b=rwkv_ln_b[i], swa_sink=swa_sink[i],
                  w_branch=w_branch[i], w_out=w_out[i], norm2_g=norm2_g[i], router_w=router_w[i],
                  router_b=router_b[i], exp_w_gu=exp_w_gu[i], exp_b_gu=exp_b_gu[i],
                  exp_w_down=exp_w_down[i], exp_b_down=exp_b_down[i])
        x, xc = trunk_layer(x, xc, c, c_ctx, rope, lp, i < DEPTH - 1)
    return rmsnorm(x, final_g)
```

```python
import functools

import numpy as np
import jax
import jax.numpy as jnp
from jax import lax
from jax.experimental import pallas as pl
from jax.experimental.pallas import tpu as pltpu

F32 = jnp.float32
BF16 = jnp.bfloat16

GRID_W = 64
ROPE_THETA = 10000.0
NORM_EPS = 1e-6
H_A, MLA_NOPE, MLA_ROPE, MLA_V = 16, 128, 64, 128
Q_LORA, KV_LORA = 1024, 512
MLA_SCALE = (MLA_NOPE + MLA_ROPE) ** -0.5
MLA_QW = 256
H_B, N_B = 32, 64
W_B = H_B * N_B
R_W, R_A, R_G = 96, 96, 256
GN_EPS = 64e-5
H_C, KV_C, HD_C = 32, 4, 64
G_C = H_C // KV_C
WINDOW = 128
W_C = H_C * HD_C
SWA_SCALE = HD_C ** -0.5
N_BRANCH = 3
N_EXPERTS, TOP_K, D_EXPERT = 32, 4, 512
SWIGLU_LIMIT, SWIGLU_ALPHA = 7.0, 1.702
MOE_BLOCK = 128
C_RWKV = 3 * W_B + 2 * R_W + 2 * R_A + R_G

LANES = 128
VMEM_LIMIT = 56 * 1024 * 1024


def _cparams(*sem):
    return pltpu.CompilerParams(dimension_semantics=sem, vmem_limit_bytes=VMEM_LIMIT)


def _tile(n, prefs):
    for t in prefs:
        if n % t == 0:
            return t
    return n


def _mm_kernel(*refs, nk, has_bias, resid, ctx_rows, tm, precision):
    it = iter(refs)
    a_ref, b_ref = next(it), next(it)
    bias_ref = next(it) if has_bias else None
    res_ref, mod_ref = (next(it), next(it)) if resid else (None, None)
    o_ref = next(it)
    acc_ref = next(it) if nk > 1 else None

    if precision is None:
        part = jnp.dot(a_ref[...].astype(BF16), b_ref[...].astype(BF16), preferred_element_type=F32)
    else:
        part = jnp.dot(a_ref[...], b_ref[...], preferred_element_type=F32, precision=precision)

    def finish(acc):
        if has_bias:
            acc = acc + bias_ref[...]
        if resid:
            rows = pl.program_id(0) * tm + lax.broadcasted_iota(jnp.int32, (tm, 1), 0)
            m = jnp.where(rows < ctx_rows, mod_ref[1:2, :], mod_ref[0:1, :])
            acc = res_ref[...] + m * acc
        o_ref[...] = acc.astype(o_ref.dtype)

    if nk == 1:
        finish(part)
    else:
        k = pl.program_id(2)

        @pl.when(k == 0)
        def _():
            acc_ref[...] = part

        @pl.when(k > 0)
        def _():
            acc_ref[...] += part

        @pl.when(k == nk - 1)
        def _():
            finish(acc_ref[...])


def _matmul(a, b, *, a_col_off=0, k_dim=None, bias=None, resid=None, mod=None, ctx_rows=0,
            out_dtype=F32, tm=None, tn=None, tk=None, precision=None, name="matmul"):
    M = a.shape[0]
    K, N = b.shape
    if k_dim is None:
        assert a.shape[1] == K and a_col_off == 0
    tm = tm or _tile(M, (640, 512, 256, 128))
    tn = tn or _tile(N, (1024, 896, 512, 256, 128))
    tk = tk or K
    assert M % tm == 0 and N % tn == 0 and K % tk == 0 and a_col_off % tk == 0
    nk = K // tk
    koff = a_col_off // tk
    in_specs = [pl.BlockSpec((tm, tk), lambda i, j, k: (i, k + koff)),
                pl.BlockSpec((tk, tn), lambda i, j, k: (k, j))]
    args = [a, b]
    if bias is not None:
        in_specs.append(pl.BlockSpec((1, tn), lambda i, j, k: (0, j)))
        args.append(bias.reshape(1, N).astype(F32))
    if resid is not None:
        in_specs.append(pl.BlockSpec((tm, tn), lambda i, j, k: (i, j)))
        in_specs.append(pl.BlockSpec((8, tn), lambda i, j, k: (0, j)))
        args += [resid, mod]
    kern = functools.partial(_mm_kernel, nk=nk, has_bias=bias is not None, resid=resid is not None,
                             ctx_rows=ctx_rows, tm=tm, precision=precision)
    return pl.pallas_call(
        kern,
        grid=(M // tm, N // tn, nk),
        in_specs=in_specs,
        out_specs=pl.BlockSpec((tm, tn), lambda i, j, k: (i, j)),
        out_shape=jax.ShapeDtypeStruct((M, N), out_dtype),
        scratch_shapes=[pltpu.VMEM((tm, tn), F32)] if nk > 1 else [],
        compiler_params=_cparams("parallel", "parallel", "arbitrary"),
        name=name,
    )(*args)


def _norm_kernel(*refs, eps, modulated, ctx_rows, tm):
    if modulated:
        x_ref, g_ref, sh_ref, sc_ref, o_ref = refs
    else:
        x_ref, g_ref, o_ref = refs
    x = x_ref[...].astype(F32)
    y = x * lax.rsqrt(jnp.mean(x * x, axis=-1, keepdims=True) + eps) * g_ref[...]
    if modulated:
        rows = pl.program_id(0) * tm + lax.broadcasted_iota(jnp.int32, (tm, 1), 0)
        isctx = rows < ctx_rows
        sc = jnp.where(isctx, sc_ref[1:2, :], sc_ref[0:1, :])
        sh = jnp.where(isctx, sh_ref[1:2, :], sh_ref[0:1, :])
        y = y * (1.0 + sc) + sh
    o_ref[...] = y.astype(o_ref.dtype)


def _rownorm(x, g, *, col_off=0, width=None, row_off=0, rows=None, shift=None, scale=None,
             ctx_rows=0, out_dtype=BF16, eps=NORM_EPS, name="rownorm"):
    width = width or x.shape[1]
    rows = rows or x.shape[0]
    tm = _tile(rows, (256, 128))
    assert col_off % width == 0 and row_off % tm == 0 and rows % tm == 0
    cb, rb = col_off // width, row_off // tm
    modulated = shift is not None
    in_specs = [pl.BlockSpec((tm, width), lambda i: (i + rb, cb)),
                pl.BlockSpec((1, width), lambda i: (0, 0))]
    args = [x, g.reshape(1, width).astype(F32)]
    if modulated:
        in_specs += [pl.BlockSpec((8, width), lambda i: (0, 0))] * 2
        args += [shift, scale]
    kern = functools.partial(_norm_kernel, eps=eps, modulated=modulated, ctx_rows=ctx_rows, tm=tm)
    return pl.pallas_call(
        kern,
        grid=(rows // tm,),
        in_specs=in_specs,
        out_specs=pl.BlockSpec((tm, width), lambda i: (i, 0)),
        out_shape=jax.ShapeDtypeStruct((rows, width), out_dtype),
        compiler_params=_cparams("parallel"),
        name=name,
    )(*args)


def _flash_kernel(q_ref, kn_ref, kr_ref, v_ref, o_ref, m_ref, l_ref, acc_ref, *, nk):
    kj = pl.program_id(2)

    @pl.when(kj == 0)
    def _():
        m_ref[...] = jnp.full_like(m_ref, -jnp.inf)
        l_ref[...] = jnp.zeros_like(l_ref)
        acc_ref[...] = jnp.zeros_like(acc_ref)

    k = jnp.concatenate([kn_ref[...], kr_ref[...]], axis=-1)
    s = lax.dot_general(q_ref[...], k, (((1,), (1,)), ((), ())), preferred_element_type=F32)
    m_prev = m_ref[...]
    m_new = jnp.maximum(m_prev, jnp.max(s, axis=-1, keepdims=True))
    alpha = jnp.exp(m_prev - m_new)
    p = jnp.exp(s - m_new)
    l_ref[...] = alpha * l_ref[...] + jnp.sum(p, axis=-1, keepdims=True)
    acc_ref[...] = alpha * acc_ref[...] + jnp.dot(p.astype(BF16), v_ref[...], preferred_element_type=F32)
    m_ref[...] = m_new

    @pl.when(kj == nk - 1)
    def _():
        o_ref[...] = (acc_ref[...] / l_ref[...]).astype(o_ref.dtype)


def _mla_attention(q, kv, kr, *, n_q, n_keys, name):
    tq = _tile(n_q, (1024, 512, 256, 128))
    tk = _tile(n_keys, (1280, 640, 512, 256, 128))
    nk = n_keys // tk
    return pl.pallas_call(
        functools.partial(_flash_kernel, nk=nk),
        grid=(H_A, n_q // tq, nk),
        in_specs=[pl.BlockSpec((tq, MLA_QW), lambda h, i, j: (i, h)),
                  pl.BlockSpec((tk, MLA_NOPE), lambda h, i, j: (j, 2 * h)),
                  pl.BlockSpec((tk, LANES), lambda h, i, j: (j, 0)),
                  pl.BlockSpec((tk, MLA_V), lambda h, i, j: (j, 2 * h + 1))],
        out_specs=pl.BlockSpec((tq, MLA_V), lambda h, i, j: (i, h)),
        out_shape=jax.ShapeDtypeStruct((n_q, H_A * MLA_V), BF16),
        scratch_shapes=[pltpu.VMEM((tq, 1), F32), pltpu.VMEM((tq, 1), F32), pltpu.VMEM((tq, MLA_V), F32)],
        compiler_params=_cparams("parallel", "parallel", "arbitrary"),
        name=name,
    )(q, kv, kr, kv)


def _swa_kernel(*refs, band, nb):
    if band:
        q_ref, sink_ref, kc_ref, vc_ref, kp_ref, k0_ref, kn_ref, vp_ref, v0_ref, vn_ref, o_ref = refs
    else:
        q_ref, sink_ref, kc_ref, vc_ref, o_ref = refs
    i = pl.program_id(1)
    qb = q_ref.shape[1]
    q = q_ref[...].reshape(G_C * qb, HD_C)
    sink = sink_ref[0]

    def scores(k):
        return lax.dot_general(q, k, (((1,), (1,)), ((), ())), preferred_element_type=F32)

    parts = [(scores(kc_ref[0]), vc_ref[0])]
    if band:
        r = lax.broadcasted_iota(jnp.int32, (qb, qb), 0)
        c = lax.broadcasted_iota(jnp.int32, (qb, qb), 1)

        def masked(s, ok):
            s3 = s.reshape(G_C, qb, qb)
            return jnp.where(ok[None], s3, -jnp.inf).reshape(G_C * qb, qb)

        no_prev = jnp.where(i > 0, 0, -4 * qb)
        no_next = jnp.where(i < nb - 1, 0, -4 * qb)
        parts.append((masked(scores(kp_ref[0]), c - r + no_prev >= qb - WINDOW), vp_ref[0]))
        parts.append((scores(k0_ref[0]), v0_ref[0]))
        parts.append((masked(scores(kn_ref[0]), r - c + no_next >= qb - WINDOW), vn_ref[0]))
    m = sink
    for s, _ in parts:
        m = jnp.maximum(m, jnp.max(s, axis=-1, keepdims=True))
    l = jnp.exp(sink - m)
    acc = jnp.zeros((G_C * qb, HD_C), F32)
    for s, v in parts:
        p = jnp.exp(s - m)
        l = l + jnp.sum(p, axis=-1, keepdims=True)
        acc = acc + jnp.dot(p.astype(BF16), v, preferred_element_type=F32)
    o_ref[...] = (acc / l).reshape(G_C, qb, HD_C).astype(o_ref.dtype)


def _swa_attention(q3, k3, v3, sink_rows, *, n_ctx, band, name):
    T = q3.shape[1]
    qb = 128
    assert WINDOW == qb and n_ctx % qb == 0
    cb = n_ctx // qb
    if band:
        nb = (T - n_ctx) // qb
        q_map = lambda g, i: (g, i + cb, 0)
    else:
        nb = cb
        q_map = lambda g, i: (g, i, 0)
    in_specs = [pl.BlockSpec((G_C, qb, HD_C), q_map),
                pl.BlockSpec((1, G_C * qb, 1), lambda g, i: (g, 0, 0)),
                pl.BlockSpec((1, n_ctx, HD_C), lambda g, i: (g, 0, 0)),
                pl.BlockSpec((1, n_ctx, HD_C), lambda g, i: (g, 0, 0))]
    args = [q3, sink_rows, k3, v3]
    if band:
        blk = lambda off: pl.BlockSpec(
            (1, qb, HD_C), lambda g, i: (g, cb + jnp.clip(i + off, 0, nb - 1), 0))
        in_specs += [blk(-1), blk(0), blk(1)] * 2
        args += [k3, k3, k3, v3, v3, v3]
    return pl.pallas_call(
        functools.partial(_swa_kernel, band=band, nb=nb),
        grid=(KV_C, nb),
        in_specs=in_specs,
        out_specs=pl.BlockSpec((G_C, qb, HD_C), lambda g, i: (g, i, 0)),
        out_shape=jax.ShapeDtypeStruct((H_C, nb * qb, HD_C), BF16),
        compiler_params=_cparams("parallel", "parallel"),
        name=name,
    )(*args)


N_PAIR = H_B // 2


def _scan_kernel(r_ref, v_ref, kk_ref, w_ref, kd_ref, b_ref, o_ref, s_ref, *, tb):
    d = pl.program_id(0)
    j = pl.program_id(1)

    @pl.when(j == 0)
    def _():
        s_ref[...] = jnp.zeros_like(s_ref)

    seg = (lax.broadcasted_iota(jnp.int32, (LANES, LANES), 0) // N_B
           == lax.broadcasted_iota(jnp.int32, (LANES, LANES), 1) // N_B).astype(BF16)
    diag = (lax.broadcasted_iota(jnp.int32, (N_B, LANES), 0)
            == lax.broadcasted_iota(jnp.int32, (N_B, LANES), 1) % N_B).astype(F32)

    def segsum(x, terms):
        out = None
        for _ in range(terms):
            hi = x.astype(BF16)
            y = jnp.dot(hi, seg, preferred_element_type=F32)
            out = y if out is None else out + y
            x = x - hi.astype(F32)
        return out

    def step(i, carry):
        t = jnp.where(d == 0, i, tb - 1 - i)
        r_t, v_t, kk_t = r_ref[t], v_ref[t], kk_ref[t]
        w_t, kd_t, b_t = w_ref[0, t], kd_ref[0, t], b_ref[0, t]
        outs = []
        for p in range(N_PAIR):
            row = lambda x: x[p:p + 1, :]
            s = s_ref[p]
            sa = segsum(s * row(kk_t), 2)
            vcol = segsum(diag * row(v_t), 3)
            s = s * row(w_t) - sa * row(b_t) + vcol * row(kd_t)
            s_ref[p] = s
            ob = segsum(s * row(r_t), 2)
            outs.append(jnp.sum(diag * ob, axis=0, keepdims=True))
        o_ref[0, t] = jnp.concatenate(outs, axis=0)
        return carry

    lax.fori_loop(0, tb, step, 0)


def _rwkv_scan(r, v, kk, w2, kd2, b2, *, n_ctx):
    T = r.shape[0]
    tb = 128
    nb, cb = T // tb, n_ctx // tb
    assert T % tb == 0 and n_ctx % tb == 0

    def blk(d, j):
        back = jnp.where(j < cb, cb - 1 - j, nb - 1 - (j - cb))
        return jnp.where(d == 0, j, back)

    shared = pl.BlockSpec((tb, N_PAIR, LANES), lambda d, j: (blk(d, j), 0, 0))
    perdir = pl.BlockSpec((1, tb, N_PAIR, LANES), lambda d, j: (d, blk(d, j), 0, 0))
    return pl.pallas_call(
        functools.partial(_scan_kernel, tb=tb),
        grid=(2, nb),
        in_specs=[shared, shared, shared, perdir, perdir, perdir],
        out_specs=perdir,
        out_shape=jax.ShapeDtypeStruct((2, T, N_PAIR, LANES), F32),
        scratch_shapes=[pltpu.VMEM((N_PAIR, N_B, LANES), F32)],
        compiler_params=_cparams("arbitrary", "arbitrary"),
        name="rwkv_scan",
    )(r, v, kk, w2, kd2, b2)


def _merge_kernel(o_ref, w_ref, g_ref, out_ref, acc_ref):
    j = pl.program_id(2)
    part = jax.nn.sigmoid(g_ref[...]) * jnp.dot(o_ref[0], w_ref[0], preferred_element_type=F32)

    @pl.when(j == 0)
    def _():
        acc_ref[...] = part

    @pl.when(j > 0)
    def _():
        acc_ref[...] += part

    @pl.when(j == N_BRANCH - 1)
    def _():
        out_ref[...] = acc_ref[...].astype(out_ref.dtype)


def _merge(o3, w_branch, p, gate_off):
    _, T, W = o3.shape
    D = w_branch.shape[2]
    tm = _tile(T, (640, 512, 256, 128))
    tn = _tile(D, (1024, 512, 256, 128))
    assert gate_off % tn == 0
    gb, nj = gate_off // tn, D // tn
    return pl.pallas_call(
        _merge_kernel,
        grid=(T // tm, D // tn, N_BRANCH),
        in_specs=[pl.BlockSpec((1, tm, W), lambda i, n, j: (j, i, 0)),
                  pl.BlockSpec((1, W, tn), lambda i, n, j: (j, 0, n)),
                  pl.BlockSpec((tm, tn), lambda i, n, j: (i, gb + j * nj + n))],
        out_specs=pl.BlockSpec((tm, tn), lambda i, n, j: (i, n)),
        out_shape=jax.ShapeDtypeStruct((T, D), BF16),
        scratch_shapes=[pltpu.VMEM((tm, tn), F32)],
        compiler_params=_cparams("parallel", "parallel", "arbitrary"),
        name="merge",
    )(o3, w_branch, p)


def _router_kernel(x_ref, w_ref, b_ref, e_ref, g_ref):
    logits = jnp.dot(x_ref[...], w_ref[...], preferred_element_type=F32,
                     precision=lax.Precision.HIGHEST) + b_ref[...]
    lane = lax.broadcasted_iota(jnp.int32, logits.shape, 1)
    vals, idxs = [], []
    x = logits
    for _ in range(TOP_K):
        m = jnp.max(x, axis=-1, keepdims=True)
        idx = jnp.min(jnp.where(x == m, lane, LANES), axis=-1, keepdims=True)
        vals.append(m)
        idxs.append(idx)
        x = jnp.where(lane == idx, -jnp.inf, x)
    ex = [jnp.exp(v - vals[0]) for v in vals]
    tot = ex[0] + ex[1] + ex[2] + ex[3]
    e_out = jnp.zeros(logits.shape, jnp.int32)
    g_out = jnp.zeros(logits.shape, F32)
    for k in range(TOP_K):
        e_out = jnp.where(lane == k, idxs[k], e_out)
        g_out = jnp.where(lane == k, ex[k] / tot, g_out)
    e_ref[...] = e_out
    g_ref[...] = g_out


def _router(h, router_w, router_b):
    T, D = h.shape
    tm = _tile(T, (256, 128))
    w = jnp.zeros((D, LANES), F32).at[:, :N_EXPERTS].set(router_w)
    b = jnp.full((1, LANES), -1e30, F32).at[0, :N_EXPERTS].set(router_b)
    return pl.pallas_call(
        _router_kernel,
        grid=(T // tm,),
        in_specs=[pl.BlockSpec((tm, D), lambda i: (i, 0)),
                  pl.BlockSpec((D, LANES), lambda i: (0, 0)),
                  pl.BlockSpec((1, LANES), lambda i: (0, 0))],
        out_specs=[pl.BlockSpec((tm, LANES), lambda i: (i, 0))] * 2,
        out_shape=[jax.ShapeDtypeStruct((T, LANES), jnp.int32), jax.ShapeDtypeStruct((T, LANES), F32)],
        compiler_params=_cparams("parallel"),
        name="router_topk",
    )(h, w, b)


GATHER_ROWS = 128


def _gather_kernel(idx_ref, src_ref, o_ref, sem):
    def row_copy(r):
        return pltpu.make_async_copy(src_ref.at[pl.ds(idx_ref[0, 0, r], 1), :], o_ref.at[pl.ds(r, 1), :], sem)

    def start(r, c):
        row_copy(r).start()
        return c

    def wait(r, c):
        row_copy(r).wait()
        return c

    lax.fori_loop(0, GATHER_ROWS, start, 0)
    lax.fori_loop(0, GATHER_ROWS, wait, 0)


def _gather_rows(src, idx):
    n = idx.shape[0]
    D = src.shape[1]
    assert n % GATHER_ROWS == 0
    nb = n // GATHER_ROWS
    return pl.pallas_call(
        _gather_kernel,
        grid=(nb,),
        in_specs=[pl.BlockSpec((1, 1, GATHER_ROWS), lambda i: (i, 0, 0), memory_space=pltpu.SMEM),
                  pl.BlockSpec(memory_space=pl.ANY)],
        out_specs=pl.BlockSpec((GATHER_ROWS, D), lambda i: (i, 0)),
        out_shape=jax.ShapeDtypeStruct((n, D), src.dtype),
        scratch_shapes=[pltpu.SemaphoreType.DMA(())],
        compiler_params=_cparams("arbitrary"),
        name="gather_rows",
    )(idx.reshape(nb, 1, GATHER_ROWS), src)


def _expert_kernel(be_ref, nu_ref, x_ref, g_ref, wgu_ref, bgu_ref, wd_ref, bd_ref, o_ref):
    i = pl.program_id(0)

    @pl.when(i < nu_ref[0])
    def _():
        gu = jnp.dot(x_ref[...].astype(BF16), wgu_ref[0], preferred_element_type=F32) + bgu_ref[0]
        gate = jnp.minimum(gu[:, :D_EXPERT], SWIGLU_LIMIT)
        up = jnp.clip(gu[:, D_EXPERT:], -SWIGLU_LIMIT, SWIGLU_LIMIT)
        act = (up + 1.0) * gate * jax.nn.sigmoid(gate * SWIGLU_ALPHA)
        y = jnp.dot(act.astype(BF16), wd_ref[0], preferred_element_type=F32) + bd_ref[0]
        o_ref[...] = y * g_ref[...]

    @pl.when(i >= nu_ref[0])
    def _():
        o_ref[...] = jnp.zeros_like(o_ref)


def _experts(xg, row_gate, block_e, n_used, w_gu, b_gu, w_down, b_down):
    n_rows, D = xg.shape
    nb = n_rows // MOE_BLOCK
    grid_spec = pltpu.PrefetchScalarGridSpec(
        num_scalar_prefetch=2,
        grid=(nb,),
        in_specs=[pl.BlockSpec((MOE_BLOCK, D), lambda i, be, nu: (i, 0)),
                  pl.BlockSpec((MOE_BLOCK, 1), lambda i, be, nu: (i, 0)),
                  pl.BlockSpec((1, D, 2 * D_EXPERT), lambda i, be, nu: (be[i], 0, 0)),
                  pl.BlockSpec((1, 1, 2 * D_EXPERT), lambda i, be, nu: (be[i], 0, 0)),
                  pl.BlockSpec((1, D_EXPERT, D), lambda i, be, nu: (be[i], 0, 0)),
                  pl.BlockSpec((1, 1, D), lambda i, be, nu: (be[i], 0, 0))],
        out_specs=pl.BlockSpec((MOE_BLOCK, D), lambda i, be, nu: (i, 0)),
    )
    return pl.pallas_call(
        _expert_kernel,
        grid_spec=grid_spec,
        out_shape=jax.ShapeDtypeStruct((n_rows, D), F32),
        compiler_params=_cparams("arbitrary"),
        name="experts",
    )(block_e, n_used, xg, row_gate, w_gu, b_gu.reshape(N_EXPERTS, 1, -1), w_down,
      b_down.reshape(N_EXPERTS, 1, -1))


def _combine_kernel(y_ref, res_ref, mod_ref, o_ref, *, ctx_rows, tm, D):
    y = y_ref[:, 0:D]
    for k in range(1, TOP_K):
        y = y + y_ref[:, k * D:(k + 1) * D]
    rows = pl.program_id(0) * tm + lax.broadcasted_iota(jnp.int32, (tm, 1), 0)
    m = jnp.where(rows < ctx_rows, mod_ref[1:2, :], mod_ref[0:1, :])
    o_ref[...] = res_ref[...] + m * y


def _combine(yg, resid, mod, ctx_rows):
    T, D = resid.shape
    tm = _tile(T, (128,))
    return pl.pallas_call(
        functools.partial(_combine_kernel, ctx_rows=ctx_rows, tm=tm, D=D),
        grid=(T // tm,),
        in_specs=[pl.BlockSpec((tm, TOP_K * D), lambda i: (i, 0)),
                  pl.BlockSpec((tm, D), lambda i: (i, 0)),
                  pl.BlockSpec((8, D), lambda i: (0, 0))],
        out_specs=pl.BlockSpec((tm, D), lambda i: (i, 0)),
        out_shape=jax.ShapeDtypeStruct((T, D), F32),
        compiler_params=_cparams("parallel"),
        name="moe_combine",
    )(yg.reshape(T, TOP_K * D), resid, mod)


def _moe(h2, resid, mod, ctx_rows, router_w, router_b, w_gu, b_gu, w_down, b_down):
    T, D = h2.shape
    e128, g128 = _router(h2, router_w, router_b)
    top_e, gates = e128[:, :TOP_K], g128[:, :TOP_K]
    onehot = (top_e[:, :, None] == jnp.arange(N_EXPERTS, dtype=jnp.int32)).astype(jnp.int32).sum(1)
    counts = onehot.sum(0)
    rank = jnp.cumsum(onehot, axis=0) - onehot
    pcounts = (counts + MOE_BLOCK - 1) // MOE_BLOCK * MOE_BLOCK
    pends = jnp.cumsum(pcounts)
    pstarts = pends - pcounts
    dest = (pstarts[top_e] + jnp.take_along_axis(rank, top_e, axis=1)).astype(jnp.int32)
    n_blocks = -(-T * TOP_K // MOE_BLOCK) + N_EXPERTS
    n_rows = n_blocks * MOE_BLOCK
    flat_dest = dest.reshape(-1)
    tok = jnp.repeat(jnp.arange(T, dtype=jnp.int32), TOP_K)
    row_tok = jnp.zeros((n_rows,), jnp.int32).at[flat_dest].set(tok)
    row_gate = jnp.zeros((n_rows,), F32).at[flat_dest].set(gates.reshape(-1))
    block_e = jnp.minimum(
        jnp.searchsorted(pends, jnp.arange(n_blocks, dtype=pends.dtype) * MOE_BLOCK, side='right'),
        N_EXPERTS - 1).astype(jnp.int32)
    n_used = (pends[-1] // MOE_BLOCK).astype(jnp.int32).reshape(1)
    xg = _gather_rows(h2, row_tok)
    yb = _experts(xg, row_gate.reshape(n_rows, 1), block_e, n_used, w_gu, b_gu, w_down, b_down)
    yg = _gather_rows(yb, flat_dest)
    return _combine(yg, resid, mod, ctx_rows)


def _rope_tables(n_ctx, S):
    half = MLA_ROPE // 2
    inv = ROPE_THETA ** (-jnp.arange(0, half, 2, dtype=F32) / half)
    rows = jnp.repeat(jnp.arange(S // GRID_W), GRID_W).astype(F32)[:, None] * inv
    cols = jnp.tile(jnp.arange(GRID_W), S // GRID_W).astype(F32)[:, None] * inv
    ang = jnp.concatenate([rows, rows, cols, cols], -1)
    cos = jnp.concatenate([jnp.ones((n_ctx, MLA_ROPE), F32), jnp.cos(ang)], 0)
    sin = jnp.concatenate([jnp.zeros((n_ctx, MLA_ROPE), F32), jnp.sin(ang)], 0)
    return cos, sin


def _rope(x, cos, sin):
    q = MLA_ROPE // 4

    def rot(p):
        return jnp.concatenate([-p[..., q:], p[..., :q]], -1)

    xr = jnp.concatenate([rot(x[..., :2 * q]), rot(x[..., 2 * q:])], -1)
    return x * cos[:, None, :] + xr * sin[:, None, :]


def _pad_k(a, k):
    return jnp.pad(a, ((0, 0), (0, k - a.shape[1])))


def _lora(a, w, name):
    kp = -(-a.shape[1] // LANES) * LANES
    return _matmul(_pad_k(a, kp).astype(BF16), jnp.pad(w, ((0, kp - w.shape[0]), (0, 0))).astype(BF16), name=name)


def _seg_shift(p, mu, n_ctx):
    def one(s):
        prev = jnp.pad(s, ((1, 0), (0, 0)))[:-1]
        nxt = jnp.pad(s, ((0, 1), (0, 0)))[1:]
        return s + mu[0] * (prev - s) + mu[1] * (nxt - s)

    return jnp.concatenate([one(p[:n_ctx]), one(p[n_ctx:])], 0)


def _rwkv_mixer(p_rw, lp, n_ctx):
    T = p_rw.shape[0]
    ps = _seg_shift(p_rw, lp['rwkv_mu'], n_ctx)
    o = 0
    parts = []
    for wdt in (W_B, W_B, W_B, R_W, R_W, R_A, R_A, R_G):
        parts.append(ps[:, o:o + wdt])
        o += wdt
    r, k, v, wd_f, wd_b, ad_f, ad_b, gd = parts
    heads = lambda t: t.reshape(T, H_B, N_B)
    kk = heads(k * lp['rwkv_k_k'])
    kk = (kk * lax.rsqrt(jnp.sum(kk * kk, -1, keepdims=True) + 1e-12)).reshape(T, W_B)
    ws, kds, bs = [], [], []
    for d, (wd, ad) in enumerate(((wd_f, ad_f), (wd_b, ad_b))):
        w_pre = lp['rwkv_w0'][d] + _lora(jnp.tanh(wd), lp['rwkv_w_up'][d], "rwkv_w_lora")
        decay = jnp.exp(-jnp.exp(-jax.nn.softplus(-w_pre) - 0.5))
        a = jax.nn.sigmoid(lp['rwkv_a0'][d] + _lora(ad, lp['rwkv_a_up'][d], "rwkv_a_lora"))
        kd = k * (1.0 + (a - 1.0) * lp['rwkv_k_a'])
        ws.append(decay)
        kds.append(kd)
        bs.append(kk * a)
    g = _lora(jax.nn.sigmoid(gd), lp['rwkv_g_up'], "rwkv_g_lora")
    pk = lambda t: t.reshape(T, N_PAIR, LANES)
    pk2 = lambda ts: jnp.stack(ts, 0).reshape(2, T, N_PAIR, LANES)
    o2 = _rwkv_scan(pk(r), pk(v), pk(kk), pk2(ws), pk2(kds), pk2(bs), n_ctx=n_ctx)
    o = heads(o2[0] + o2[1])
    mu = jnp.mean(o, -1, keepdims=True)
    var = jnp.mean(jnp.square(o - mu), -1, keepdims=True)
    on = ((o - mu) * lax.rsqrt(var + GN_EPS)).reshape(T, W_B) * lp['rwkv_ln_g'] + lp['rwkv_ln_b']
    r_k = lp['rwkv_r_k'].reshape(H_B, N_B)
    rh = heads(r)
    coef = (jnp.sum(rh * heads(kds[0]) * r_k, -1, keepdims=True)
            + jnp.sum(rh * heads(kds[1]) * r_k, -1, keepdims=True))
    bonus = (coef * heads(v)).reshape(T, W_B)
    return ((on + bonus) * g).astype(BF16)


def _in_layout(D):
    o = {}
    o['g'] = 0
    o['q'] = N_BRANCH * D
    o['kv'] = o['q'] + Q_LORA
    o['rw'] = o['kv'] + KV_LORA
    o['sq'] = o['rw'] + C_RWKV
    o['sk'] = o['sq'] + W_C
    o['sv'] = o['sk'] + KV_C * HD_C
    o['kr'] = o['sv'] + KV_C * HD_C
    return o


def _prep_w_in(w_in, D):
    s = np.cumsum([0, Q_LORA, KV_LORA, MLA_ROPE, C_RWKV, W_C, KV_C * HD_C, KV_C * HD_C, N_BRANCH * D])
    seg = lambda i: w_in[:, int(s[i]):int(s[i + 1])]
    pad = (-(int(s[-1])) % (2 * LANES))
    cols = [seg(7), seg(0), seg(1), seg(3), seg(4), seg(5), seg(6), seg(2)]
    if pad:
        cols.append(jnp.zeros((w_in.shape[0], pad), w_in.dtype))
    return jnp.concatenate(cols, -1).astype(BF16)


def _prep_wq_up(wq_up):
    w = wq_up.reshape(Q_LORA, H_A, MLA_NOPE + MLA_ROPE)
    w = jnp.pad(w, ((0, 0), (0, 0), (0, MLA_QW - MLA_NOPE - MLA_ROPE)))
    return w.reshape(Q_LORA, H_A * MLA_QW).astype(BF16)


def _layer(xall, mod, lp, rope, n_ctx):
    T, D = xall.shape
    S = T - n_ctx
    cos, sin = rope
    md = [mod[:, j * D:(j + 1) * D] for j in range(6)]
    lo = _in_layout(D)

    h1 = _rownorm(xall, lp['norm1_g'], shift=md[0], scale=md[1], ctx_rows=n_ctx, name="norm1")
    w_in = _prep_w_in(lp['w_in'], D)
    p = _matmul(h1, w_in, tm=_tile(T, (1280, 640, 256, 128)), tn=_tile(w_in.shape[1], (896, 512, 256, 128)),
                name="w_in")

    qn = _rownorm(p, lp['mla_qn_g'], col_off=lo['q'], width=Q_LORA, name="mla_qnorm")
    q = _matmul(qn, _prep_wq_up(lp['mla_wq_up']), name="mla_q_up").reshape(T, H_A, MLA_QW)
    q = jnp.concatenate([q[..., :MLA_NOPE], _rope(q[..., MLA_NOPE:MLA_NOPE + MLA_ROPE], cos, sin),
                         q[..., MLA_NOPE + MLA_ROPE:]], -1)
    q = (q * MLA_SCALE).astype(BF16).reshape(T, H_A * MLA_QW)
    kvn = _rownorm(p, lp['mla_kvn_g'], col_off=lo['kv'], width=KV_LORA, name="mla_kvnorm")
    kv = _matmul(kvn, lp['mla_wkv_up'].astype(BF16), out_dtype=BF16, name="mla_kv_up")
    kr = _rope(p[:, None, lo['kr']:lo['kr'] + MLA_ROPE], cos, sin)[:, 0]
    kr = _pad_k(kr, LANES).astype(BF16)
    oa_x = _mla_attention(q[n_ctx:], kv, kr, n_q=S, n_keys=T, name="mla_latent")
    oa_c = _mla_attention(q[:n_ctx], kv, kr, n_q=n_ctx, n_keys=n_ctx, name="mla_context")
    o_a = jnp.concatenate([oa_c, oa_x], 0)

    o_b = _rwkv_mixer(p[:, lo['rw']:lo['rw'] + C_RWKV], lp, n_ctx)

    qc = _rope(p[:, lo['sq']:lo['sq'] + W_C].reshape(T, H_C, HD_C), cos, sin) * SWA_SCALE
    kc = _rope(p[:, lo['sk']:lo['sk'] + KV_C * HD_C].reshape(T, KV_C, HD_C), cos, sin)
    vc = p[:, lo['sv']:lo['sv'] + KV_C * HD_C].reshape(T, KV_C, HD_C)
    q3, k3, v3 = (jnp.swapaxes(t, 0, 1).astype(BF16) for t in (qc, kc, vc))
    sink_rows = jnp.repeat(lp['swa_sink'].astype(F32).reshape(KV_C, G_C), 128, axis=1).reshape(KV_C, G_C * 128, 1)
    oc_x = _swa_attention(q3, k3, v3, sink_rows, n_ctx=n_ctx, band=True, name="swa_latent")
    oc_c = _swa_attention(q3, k3, v3, sink_rows, n_ctx=n_ctx, band=False, name="swa_context")
    o_c = jnp.swapaxes(jnp.concatenate([oc_c, oc_x], 1), 0, 1).reshape(T, W_C)

    m = _merge(jnp.stack([o_a, o_b, o_c], 0), lp['w_branch'].astype(BF16), p, lo['g'])
    x1 = _matmul(m, lp['w_out'].astype(BF16), resid=xall, mod=md[2], ctx_rows=n_ctx, name="w_out")
    h2 = _rownorm(x1, lp['norm2_g'], shift=md[3], scale=md[4], ctx_rows=n_ctx, out_dtype=F32, name="norm2")
    return _moe(h2, x1, md[5], n_ctx, lp['router_w'], lp['router_b'], lp['exp_w_gu'].astype(BF16),
                lp['exp_b_gu'], lp['exp_w_down'].astype(BF16), lp['exp_b_down'])


def kernel(x, c, ctx, c_ctx, ada_w, ada_b, norm1_g, w_in, mla_qn_g, mla_kvn_g, mla_wq_up, mla_wkv_up, rwkv_mu, rwkv_w0, rwkv_w_up, rwkv_a0, rwkv_a_up, rwkv_g_up, rwkv_k_k, rwkv_k_a, rwkv_r_k, rwkv_ln_g, rwkv_ln_b, swa_sink, w_branch, w_out, norm2_g, router_w, router_b, exp_w_gu, exp_b_gu, exp_w_down, exp_b_down, final_g):
    B, S, D = x.shape
    n_ctx = ctx.shape[1]
    assert B == 1 and c.shape[0] == 1
    depth = ada_w.shape[0]
    rope = _rope_tables(n_ctx, S)
    xall = jnp.concatenate([ctx[0], x[0]], 0)
    cond = jax.nn.silu(jnp.concatenate([c, c_ctx[None], jnp.zeros((6, D), F32)], 0))
    for i in range(depth):
        lp = dict(norm1_g=norm1_g[i], w_in=w_in[i],
                  mla_qn_g=mla_qn_g[i], mla_kvn_g=mla_kvn_g[i], mla_wq_up=mla_wq_up[i], mla_wkv_up=mla_wkv_up[i],
                  rwkv_mu=rwkv_mu[i], rwkv_w0=rwkv_w0[i], rwkv_w_up=rwkv_w_up[i], rwkv_a0=rwkv_a0[i],
                  rwkv_a_up=rwkv_a_up[i], rwkv_g_up=rwkv_g_up[i], rwkv_k_k=rwkv_k_k[i], rwkv_k_a=rwkv_k_a[i],
                  rwkv_r_k=rwkv_r_k[i], rwkv_ln_g=rwkv_ln_g[i], rwkv_ln_b=rwkv_ln_b[i], swa_sink=swa_sink[i],
                  w_branch=w_branch[i], w_out=w_out[i], norm2_g=norm2_g[i], router_w=router_w[i],
                  router_b=router_b[i], exp_w_gu=exp_w_gu[i], exp_b_gu=exp_b_gu[i],
                  exp_w_down=exp_w_down[i], exp_b_down=exp_b_down[i])
        mod = _matmul(cond, ada_w[i], bias=ada_b[i], tm=8, tn=512, name="adaln")
        xall = _layer(xall, mod, lp, rope, n_ctx)
    out = _rownorm(xall, final_g, row_off=n_ctx, rows=S, out_dtype=F32, name="final_norm")
    return out.reshape(B, S, D)
```

```python
import functools

import numpy as np
import jax
import jax.numpy as jnp
from jax import lax
from jax.experimental import pallas as pl
from jax.experimental.pallas import tpu as pltpu

F32 = jnp.float32
BF16 = jnp.bfloat16

GRID_W = 64
ROPE_THETA = 10000.0
NORM_EPS = 1e-6
H_A, MLA_NOPE, MLA_ROPE, MLA_V = 16, 128, 64, 128
Q_LORA, KV_LORA = 1024, 512
MLA_SCALE = (MLA_NOPE + MLA_ROPE) ** -0.5
LOG2E = 1.4426950408889634
MLA_QW = 256
H_B, N_B = 32, 64
W_B = H_B * N_B
R_W, R_A, R_G = 96, 96, 256
GN_EPS = 64e-5
H_C, KV_C, HD_C = 32, 4, 64
G_C = H_C // KV_C
WINDOW = 128
W_C = H_C * HD_C
SWA_SCALE = HD_C ** -0.5
N_BRANCH = 3
N_EXPERTS, TOP_K, D_EXPERT = 32, 4, 512
SWIGLU_LIMIT, SWIGLU_ALPHA = 7.0, 1.702
MOE_BLOCK = 128
C_RWKV = 3 * W_B + 2 * R_W + 2 * R_A + R_G

LANES = 128
VMEM_LIMIT = 56 * 1024 * 1024


def _cparams(*sem):
    return pltpu.CompilerParams(dimension_semantics=sem, vmem_limit_bytes=VMEM_LIMIT)


def _lane_tile(x, k):
    return x if k == 1 else jnp.concatenate([x] * k, axis=1)


def _tile(n, prefs):
    for t in prefs:
        if n % t == 0:
            return t
    return n


def _mm_kernel(*refs, nk, has_bias, resid, has_tab, ctx_rows, tm, precision):
    it = iter(refs)
    a_ref, b_ref = next(it), next(it)
    bias_ref = next(it) if has_bias else None
    res_ref, mod_ref = (next(it), next(it)) if resid else (None, None)
    tab_ref = next(it) if has_tab else None
    o_ref = next(it)
    acc_ref = next(it) if nk > 1 else None

    if precision is None:
        part = jnp.dot(a_ref[...].astype(BF16), b_ref[...].astype(BF16), preferred_element_type=F32)
    else:
        part = jnp.dot(a_ref[...], b_ref[...], preferred_element_type=F32, precision=precision)

    def finish(acc):
        if has_bias:
            acc = acc + bias_ref[...]
        if resid:
            rows = pl.program_id(0) * tm + lax.broadcasted_iota(jnp.int32, (tm, 1), 0)
            m = jnp.where(rows < ctx_rows, mod_ref[1:2, :], mod_ref[0:1, :])
            acc = res_ref[...] + m * acc
        if has_tab:
            acc = acc * _lane_tile(tab_ref[...], acc.shape[1] // tab_ref.shape[1])
        o_ref[...] = acc.astype(o_ref.dtype)

    if nk == 1:
        finish(part)
    else:
        k = pl.program_id(2)

        @pl.when(k == 0)
        def _():
            acc_ref[...] = part

        @pl.when(k > 0)
        def _():
            acc_ref[...] += part

        @pl.when(k == nk - 1)
        def _():
            finish(acc_ref[...])


def _matmul(a, b, *, a_col_off=0, k_dim=None, bias=None, resid=None, mod=None, ctx_rows=0, row_table=None,
            out_dtype=F32, tm=None, tn=None, tk=None, precision=None, name="matmul"):
    M = a.shape[0]
    K, N = b.shape
    if k_dim is None:
        assert a.shape[1] == K and a_col_off == 0
    tm = tm or _tile(M, (640, 512, 256, 128))
    tn = tn or _tile(N, (1024, 896, 512, 256, 128))
    tk = tk or K
    assert M % tm == 0 and N % tn == 0 and K % tk == 0 and a_col_off % tk == 0
    nk = K // tk
    koff = a_col_off // tk
    in_specs = [pl.BlockSpec((tm, tk), lambda i, j, k: (i, k + koff)),
                pl.BlockSpec((tk, tn), lambda i, j, k: (k, j))]
    args = [a, b]
    if bias is not None:
        in_specs.append(pl.BlockSpec((1, tn), lambda i, j, k: (0, j)))
        args.append(bias.reshape(1, N).astype(F32))
    if resid is not None:
        in_specs.append(pl.BlockSpec((tm, tn), lambda i, j, k: (i, j)))
        in_specs.append(pl.BlockSpec((8, tn), lambda i, j, k: (0, j)))
        args += [resid, mod]
    if row_table is not None:
        assert tn % row_table.shape[1] == 0
        in_specs.append(pl.BlockSpec((tm, row_table.shape[1]), lambda i, j, k: (i, 0)))
        args.append(row_table)
    kern = functools.partial(_mm_kernel, nk=nk, has_bias=bias is not None, resid=resid is not None,
                             has_tab=row_table is not None, ctx_rows=ctx_rows, tm=tm, precision=precision)
    return pl.pallas_call(
        kern,
        grid=(M // tm, N // tn, nk),
        in_specs=in_specs,
        out_specs=pl.BlockSpec((tm, tn), lambda i, j, k: (i, j)),
        out_shape=jax.ShapeDtypeStruct((M, N), out_dtype),
        scratch_shapes=[pltpu.VMEM((tm, tn), F32)] if nk > 1 else [],
        compiler_params=_cparams("parallel", "parallel", "arbitrary"),
        name=name,
    )(*args)


def _norm_kernel(*refs, eps, modulated, ctx_rows, tm):
    if modulated:
        x_ref, g_ref, sh_ref, sc_ref, o_ref = refs
    else:
        x_ref, g_ref, o_ref = refs
    x = x_ref[...].astype(F32)
    y = x * lax.rsqrt(jnp.mean(x * x, axis=-1, keepdims=True) + eps) * g_ref[...]
    if modulated:
        rows = pl.program_id(0) * tm + lax.broadcasted_iota(jnp.int32, (tm, 1), 0)
        isctx = rows < ctx_rows
        sc = jnp.where(isctx, sc_ref[1:2, :], sc_ref[0:1, :])
        sh = jnp.where(isctx, sh_ref[1:2, :], sh_ref[0:1, :])
        y = y * (1.0 + sc) + sh
    o_ref[...] = y.astype(o_ref.dtype)


def _rownorm(x, g, *, col_off=0, width=None, row_off=0, rows=None, shift=None, scale=None,
             ctx_rows=0, out_dtype=BF16, eps=NORM_EPS, name="rownorm"):
    width = width or x.shape[1]
    rows = rows or x.shape[0]
    tm = _tile(rows, (256, 128))
    assert col_off % width == 0 and row_off % tm == 0 and rows % tm == 0
    cb, rb = col_off // width, row_off // tm
    modulated = shift is not None
    in_specs = [pl.BlockSpec((tm, width), lambda i: (i + rb, cb)),
                pl.BlockSpec((1, width), lambda i: (0, 0))]
    args = [x, g.reshape(1, width).astype(F32)]
    if modulated:
        in_specs += [pl.BlockSpec((8, width), lambda i: (0, 0))] * 2
        args += [shift, scale]
    kern = functools.partial(_norm_kernel, eps=eps, modulated=modulated, ctx_rows=ctx_rows, tm=tm)
    return pl.pallas_call(
        kern,
        grid=(rows // tm,),
        in_specs=in_specs,
        out_specs=pl.BlockSpec((tm, width), lambda i: (i, 0)),
        out_shape=jax.ShapeDtypeStruct((rows, width), out_dtype),
        compiler_params=_cparams("parallel"),
        name=name,
    )(*args)


def _flash_kernel(q_ref, kn_ref, kr_ref, v_ref, o_ref, m_ref, acc_ref, *, nk, sub):
    kj = pl.program_id(2)

    @pl.when(kj == 0)
    def _():
        m_ref[...] = jnp.full_like(m_ref, -jnp.inf)
        acc_ref[...] = jnp.zeros_like(acc_ref)

    q = q_ref[...]
    m, acc = m_ref[...], acc_ref[...]
    ones = jnp.ones((sub, MLA_V), BF16)
    for c in range(kn_ref.shape[0] // sub):
        rows = slice(c * sub, (c + 1) * sub)
        k = jnp.concatenate([kn_ref[rows, :], kr_ref[rows, :]], axis=-1)
        s = lax.dot_general(q, k, (((1,), (1,)), ((), ())), preferred_element_type=F32)
        m_new = jnp.maximum(m, jnp.max(s, axis=-1, keepdims=True))
        p = jnp.exp2(s - _lane_tile(m_new, sub // LANES)).astype(BF16)
        v1 = jnp.concatenate([v_ref[rows, :], ones], axis=-1)
        acc = _lane_tile(jnp.exp2(m - m_new), 2) * acc + jnp.dot(p, v1, preferred_element_type=F32)
        m = m_new
    m_ref[...] = m
    acc_ref[...] = acc

    @pl.when(kj == nk - 1)
    def _():
        acc = acc_ref[...]
        o_ref[...] = (acc[:, :MLA_V] / acc[:, MLA_V:]).astype(o_ref.dtype)


def _mla_attention(q, kv, kr, *, n_q, n_keys, name):
    tq = _tile(n_q, (1024, 512, 256, 128))
    tk = _tile(n_keys, (1280, 640, 512, 256, 128))
    nk = n_keys // tk
    sub = _tile(tk, (256, 128))
    return pl.pallas_call(
        functools.partial(_flash_kernel, nk=nk, sub=sub),
        grid=(H_A, n_q // tq, nk),
        in_specs=[pl.BlockSpec((tq, MLA_QW), lambda h, i, j: (i, h)),
                  pl.BlockSpec((tk, MLA_NOPE), lambda h, i, j: (j, 2 * h)),
                  pl.BlockSpec((tk, LANES), lambda h, i, j: (j, 0)),
                  pl.BlockSpec((tk, MLA_V), lambda h, i, j: (j, 2 * h + 1))],
        out_specs=pl.BlockSpec((tq, MLA_V), lambda h, i, j: (i, h)),
        out_shape=jax.ShapeDtypeStruct((n_q, H_A * MLA_V), BF16),
        scratch_shapes=[pltpu.VMEM((tq, LANES), F32), pltpu.VMEM((tq, 2 * MLA_V), F32)],
        compiler_params=_cparams("parallel", "parallel", "arbitrary"),
        name=name,
    )(q, kv, kr, kv)


def _swa_kernel(*refs, band, nb):
    if band:
        q_ref, sink_ref, kc_ref, vc_ref, kp_ref, k0_ref, kn_ref, vp_ref, v0_ref, vn_ref, o_ref = refs
    else:
        q_ref, sink_ref, kc_ref, vc_ref, o_ref = refs
    i = pl.program_id(1)
    qb = q_ref.shape[1]
    q = q_ref[...].reshape(G_C * qb, HD_C)
    sink = sink_ref[0]

    def scores(k):
        return lax.dot_general(q, k, (((1,), (1,)), ((), ())), preferred_element_type=F32)

    parts = [(scores(kc_ref[0]), vc_ref[0])]
    if band:
        r = lax.broadcasted_iota(jnp.int32, (qb, qb), 0)
        c = lax.broadcasted_iota(jnp.int32, (qb, qb), 1)

        def masked(s, ok):
            s3 = s.reshape(G_C, qb, qb)
            return jnp.where(ok[None], s3, -jnp.inf).reshape(G_C * qb, qb)

        no_prev = jnp.where(i > 0, 0, -4 * qb)
        no_next = jnp.where(i < nb - 1, 0, -4 * qb)
        parts.append((masked(scores(kp_ref[0]), c - r + no_prev >= qb - WINDOW), vp_ref[0]))
        parts.append((scores(k0_ref[0]), v0_ref[0]))
        parts.append((masked(scores(kn_ref[0]), r - c + no_next >= qb - WINDOW), vn_ref[0]))
    m = sink
    for s, _ in parts:
        m = jnp.maximum(m, jnp.max(s, axis=-1, keepdims=True))
    l = jnp.exp(sink - m)
    acc = jnp.zeros((G_C * qb, HD_C), F32)
    for s, v in parts:
        p = jnp.exp(s - m)
        l = l + jnp.sum(p, axis=-1, keepdims=True)
        acc = acc + jnp.dot(p.astype(BF16), v, preferred_element_type=F32)
    o_ref[...] = (acc / l).reshape(G_C, qb, HD_C).astype(o_ref.dtype)


def _swa_attention(q3, k3, v3, sink_rows, *, n_ctx, band, name):
    T = q3.shape[1]
    qb = 128
    assert WINDOW == qb and n_ctx % qb == 0
    cb = n_ctx // qb
    if band:
        nb = (T - n_ctx) // qb
        q_map = lambda g, i: (g, i + cb, 0)
    else:
        nb = cb
        q_map = lambda g, i: (g, i, 0)
    in_specs = [pl.BlockSpec((G_C, qb, HD_C), q_map),
                pl.BlockSpec((1, G_C * qb, 1), lambda g, i: (g, 0, 0)),
                pl.BlockSpec((1, n_ctx, HD_C), lambda g, i: (g, 0, 0)),
                pl.BlockSpec((1, n_ctx, HD_C), lambda g, i: (g, 0, 0))]
    args = [q3, sink_rows, k3, v3]
    if band:
        blk = lambda off: pl.BlockSpec(
            (1, qb, HD_C), lambda g, i: (g, cb + jnp.clip(i + off, 0, nb - 1), 0))
        in_specs += [blk(-1), blk(0), blk(1)] * 2
        args += [k3, k3, k3, v3, v3, v3]
    return pl.pallas_call(
        functools.partial(_swa_kernel, band=band, nb=nb),
        grid=(KV_C, nb),
        in_specs=in_specs,
        out_specs=pl.BlockSpec((G_C, qb, HD_C), lambda g, i: (g, i, 0)),
        out_shape=jax.ShapeDtypeStruct((H_C, nb * qb, HD_C), BF16),
        compiler_params=_cparams("parallel", "parallel"),
        name=name,
    )(*args)


N_PAIR = H_B // 2


def _scan_kernel(rf_ref, vf_ref, kkf_ref, wf_ref, kdf_ref, bf_ref, rb_ref, vb_ref, kkb_ref, wb_ref, kdb_ref,
                 bb_ref, of_ref, ob_ref, sf_ref, sb_ref, *, tb):
    @pl.when(pl.program_id(0) == 0)
    def _():
        sf_ref[...] = jnp.zeros_like(sf_ref)
        sb_ref[...] = jnp.zeros_like(sb_ref)

    lane_r = lax.broadcasted_iota(jnp.int32, (LANES, LANES), 0)
    lane_c = lax.broadcasted_iota(jnp.int32, (LANES, LANES), 1)
    seg = (lane_r // N_B == lane_c // N_B).astype(BF16)
    on_diag = (lax.broadcasted_iota(jnp.int32, (N_B, LANES), 0)
               == lax.broadcasted_iota(jnp.int32, (N_B, LANES), 1) % N_B)
    diag = on_diag.astype(F32)
    diag16 = on_diag.astype(BF16)

    def segsum(x):
        return jnp.dot(x, seg, preferred_element_type=F32)

    def tile_of(x, p):
        return x[p * N_B:(p + 1) * N_B, :]

    def one_step(i, tiles, s_ref):
        r_t, v_t, kk_t, w_t, kd_t, b_t = ([x[i:i + 1, :] for x in op] for op in tiles)
        sa = segsum(jnp.concatenate([(s_ref[p] * kk_t[p]).astype(BF16) for p in range(N_PAIR)], axis=0))
        vcol = segsum(jnp.concatenate(
            [diag16 * jnp.broadcast_to(v_t[p], (N_B, LANES)).astype(BF16) for p in range(N_PAIR)], axis=0))
        new = []
        for p in range(N_PAIR):
            s = s_ref[p] * w_t[p] - tile_of(sa, p) * b_t[p] + tile_of(vcol, p) * kd_t[p]
            s_ref[p] = s
            new.append((s * r_t[p]).astype(BF16))
        oc = segsum(jnp.concatenate(new, axis=0))
        return [jnp.sum(diag * tile_of(oc, p), axis=0, keepdims=True) for p in range(N_PAIR)]

    def load8(refs, row0):
        return [[ref[pl.ds(row0, 8), p * LANES:(p + 1) * LANES] for p in range(N_PAIR)] for ref in refs]

    def store8(o_ref, row0, rows):
        for p in range(N_PAIR):
            o_ref[pl.ds(row0, 8), p * LANES:(p + 1) * LANES] = jnp.concatenate([rows[i][p] for i in range(8)], 0)

    def group(g, carry):
        f0 = pl.multiple_of(g * 8, 8)
        b0 = pl.multiple_of(tb - 8 - g * 8, 8)
        f_tiles = load8((rf_ref, vf_ref, kkf_ref, wf_ref, kdf_ref, bf_ref), f0)
        b_tiles = load8((rb_ref, vb_ref, kkb_ref, wb_ref, kdb_ref, bb_ref), b0)
        f_rows, b_rows = [None] * 8, [None] * 8
        for i in range(8):
            f_rows[i] = one_step(i, f_tiles, sf_ref)
            b_rows[7 - i] = one_step(7 - i, b_tiles, sb_ref)
        store8(of_ref, f0, f_rows)
        store8(ob_ref, b0, b_rows)
        return carry

    lax.fori_loop(0, tb // 8, group, 0)


def _rwkv_scan(r, v, kk, ws, kds, bs, *, n_ctx):
    T = r.shape[0]
    tb = 128
    nb, cb = T // tb, n_ctx // tb
    assert T % tb == 0 and n_ctx % tb == 0

    def back(j):
        return jnp.where(j < cb, cb - 1 - j, nb - 1 - (j - cb))

    fwd = pl.BlockSpec((tb, W_B), lambda j: (j, 0))
    bwd = pl.BlockSpec((tb, W_B), lambda j: (back(j), 0))
    return pl.pallas_call(
        functools.partial(_scan_kernel, tb=tb),
        grid=(nb,),
        in_specs=[fwd] * 6 + [bwd] * 6,
        out_specs=[fwd, bwd],
        out_shape=[jax.ShapeDtypeStruct((T, W_B), F32)] * 2,
        scratch_shapes=[pltpu.VMEM((N_PAIR, N_B, LANES), F32)] * 2,
        compiler_params=_cparams("arbitrary"),
        name="rwkv_scan",
    )(r, v, kk, ws[0], kds[0], bs[0], r, v, kk, ws[1], kds[1], bs[1])


def _merge_kernel(o_ref, w_ref, g_ref, out_ref, acc_ref):
    j = pl.program_id(2)
    part = jax.nn.sigmoid(g_ref[...]) * jnp.dot(o_ref[0], w_ref[0], preferred_element_type=F32)

    @pl.when(j == 0)
    def _():
        acc_ref[...] = part

    @pl.when(j > 0)
    def _():
        acc_ref[...] += part

    @pl.when(j == N_BRANCH - 1)
    def _():
        out_ref[...] = acc_ref[...].astype(out_ref.dtype)


def _merge(o3, w_branch, p, gate_off):
    _, T, W = o3.shape
    D = w_branch.shape[2]
    tm = _tile(T, (640, 512, 256, 128))
    tn = _tile(D, (1024, 512, 256, 128))
    assert gate_off % tn == 0
    gb, nj = gate_off // tn, D // tn
    return pl.pallas_call(
        _merge_kernel,
        grid=(T // tm, D // tn, N_BRANCH),
        in_specs=[pl.BlockSpec((1, tm, W), lambda i, n, j: (j, i, 0)),
                  pl.BlockSpec((1, W, tn), lambda i, n, j: (j, 0, n)),
                  pl.BlockSpec((tm, tn), lambda i, n, j: (i, gb + j * nj + n))],
        out_specs=pl.BlockSpec((tm, tn), lambda i, n, j: (i, n)),
        out_shape=jax.ShapeDtypeStruct((T, D), BF16),
        scratch_shapes=[pltpu.VMEM((tm, tn), F32)],
        compiler_params=_cparams("parallel", "parallel", "arbitrary"),
        name="merge",
    )(o3, w_branch, p)


def _router_kernel(x_ref, w_ref, b_ref, e_ref, g_ref):
    logits = jnp.dot(x_ref[...], w_ref[...], preferred_element_type=F32,
                     precision=lax.Precision.HIGHEST) + b_ref[...]
    lane = lax.broadcasted_iota(jnp.int32, logits.shape, 1)
    vals, idxs = [], []
    x = logits
    for _ in range(TOP_K):
        m = jnp.max(x, axis=-1, keepdims=True)
        idx = jnp.min(jnp.where(x == m, lane, LANES), axis=-1, keepdims=True)
        vals.append(m)
        idxs.append(idx)
        x = jnp.where(lane == idx, -jnp.inf, x)
    ex = [jnp.exp(v - vals[0]) for v in vals]
    tot = ex[0] + ex[1] + ex[2] + ex[3]
    e_out = jnp.zeros(logits.shape, jnp.int32)
    g_out = jnp.zeros(logits.shape, F32)
    for k in range(TOP_K):
        e_out = jnp.where(lane == k, idxs[k], e_out)
        g_out = jnp.where(lane == k, ex[k] / tot, g_out)
    e_ref[...] = e_out
    g_ref[...] = g_out


def _router(h, router_w, router_b):
    T, D = h.shape
    tm = _tile(T, (256, 128))
    w = jnp.zeros((D, LANES), F32).at[:, :N_EXPERTS].set(router_w)
    b = jnp.full((1, LANES), -1e30, F32).at[0, :N_EXPERTS].set(router_b)
    return pl.pallas_call(
        _router_kernel,
        grid=(T // tm,),
        in_specs=[pl.BlockSpec((tm, D), lambda i: (i, 0)),
                  pl.BlockSpec((D, LANES), lambda i: (0, 0)),
                  pl.BlockSpec((1, LANES), lambda i: (0, 0))],
        out_specs=[pl.BlockSpec((tm, LANES), lambda i: (i, 0))] * 2,
        out_shape=[jax.ShapeDtypeStruct((T, LANES), jnp.int32), jax.ShapeDtypeStruct((T, LANES), F32)],
        compiler_params=_cparams("parallel"),
        name="router_topk",
    )(h, w, b)


GATHER_ROWS = 128


def _gather_kernel(idx_ref, src_ref, o_ref, sem):
    def row_copy(r):
        return pltpu.make_async_copy(src_ref.at[pl.ds(idx_ref[0, 0, r], 1), :], o_ref.at[pl.ds(r, 1), :], sem)

    def start(r, c):
        row_copy(r).start()
        return c

    def wait(r, c):
        row_copy(r).wait()
        return c

    lax.fori_loop(0, GATHER_ROWS, start, 0)
    lax.fori_loop(0, GATHER_ROWS, wait, 0)


def _gather_rows(src, idx):
    n = idx.shape[0]
    D = src.shape[1]
    assert n % GATHER_ROWS == 0
    nb = n // GATHER_ROWS
    return pl.pallas_call(
        _gather_kernel,
        grid=(nb,),
        in_specs=[pl.BlockSpec((1, 1, GATHER_ROWS), lambda i: (i, 0, 0), memory_space=pltpu.SMEM),
                  pl.BlockSpec(memory_space=pl.ANY)],
        out_specs=pl.BlockSpec((GATHER_ROWS, D), lambda i: (i, 0)),
        out_shape=jax.ShapeDtypeStruct((n, D), src.dtype),
        scratch_shapes=[pltpu.SemaphoreType.DMA(())],
        compiler_params=_cparams("arbitrary"),
        name="gather_rows",
    )(idx.reshape(nb, 1, GATHER_ROWS), src)


def _expert_kernel(be_ref, nu_ref, x_ref, g_ref, wgu_ref, bgu_ref, wd_ref, bd_ref, o_ref):
    i = pl.program_id(0)

    @pl.when(i < nu_ref[0])
    def _():
        gu = jnp.dot(x_ref[...].astype(BF16), wgu_ref[0], preferred_element_type=F32) + bgu_ref[0]
        gate = jnp.minimum(gu[:, :D_EXPERT], SWIGLU_LIMIT)
        up = jnp.clip(gu[:, D_EXPERT:], -SWIGLU_LIMIT, SWIGLU_LIMIT)
        act = (up + 1.0) * gate * jax.nn.sigmoid(gate * SWIGLU_ALPHA)
        y = jnp.dot(act.astype(BF16), wd_ref[0], preferred_element_type=F32) + bd_ref[0]
        o_ref[...] = y * g_ref[...]

    @pl.when(i >= nu_ref[0])
    def _():
        o_ref[...] = jnp.zeros_like(o_ref)


def _experts(xg, row_gate, block_e, n_used, w_gu, b_gu, w_down, b_down):
    n_rows, D = xg.shape
    nb = n_rows // MOE_BLOCK
    grid_spec = pltpu.PrefetchScalarGridSpec(
        num_scalar_prefetch=2,
        grid=(nb,),
        in_specs=[pl.BlockSpec((MOE_BLOCK, D), lambda i, be, nu: (i, 0)),
                  pl.BlockSpec((MOE_BLOCK, 1), lambda i, be, nu: (i, 0)),
                  pl.BlockSpec((1, D, 2 * D_EXPERT), lambda i, be, nu: (be[i], 0, 0)),
                  pl.BlockSpec((1, 1, 2 * D_EXPERT), lambda i, be, nu: (be[i], 0, 0)),
                  pl.BlockSpec((1, D_EXPERT, D), lambda i, be, nu: (be[i], 0, 0)),
                  pl.BlockSpec((1, 1, D), lambda i, be, nu: (be[i], 0, 0))],
        out_specs=pl.BlockSpec((MOE_BLOCK, D), lambda i, be, nu: (i, 0)),
    )
    return pl.pallas_call(
        _expert_kernel,
        grid_spec=grid_spec,
        out_shape=jax.ShapeDtypeStruct((n_rows, D), F32),
        compiler_params=_cparams("arbitrary"),
        name="experts",
    )(block_e, n_used, xg, row_gate, w_gu, b_gu.reshape(N_EXPERTS, 1, -1), w_down,
      b_down.reshape(N_EXPERTS, 1, -1))


def _combine_kernel(y0_ref, y1_ref, y2_ref, y3_ref, res_ref, mod_ref, o_ref, *, ctx_rows, tm):
    y = (y0_ref[...] + y1_ref[...]) + (y2_ref[...] + y3_ref[...])
    rows = pl.program_id(0) * tm + lax.broadcasted_iota(jnp.int32, (tm, 1), 0)
    m = jnp.where(rows < ctx_rows, mod_ref[1:2, :], mod_ref[0:1, :])
    o_ref[...] = res_ref[...] + m * y


def _combine(yg, resid, mod, ctx_rows):
    T, D = resid.shape
    tm = _tile(T, (128,))
    nt = T // tm
    assert TOP_K == 4
    return pl.pallas_call(
        functools.partial(_combine_kernel, ctx_rows=ctx_rows, tm=tm),
        grid=(nt,),
        in_specs=[pl.BlockSpec((tm, D), functools.partial(lambda i, k: (k * nt + i, 0), k=k)) for k in range(TOP_K)]
        + [pl.BlockSpec((tm, D), lambda i: (i, 0)), pl.BlockSpec((8, D), lambda i: (0, 0))],
        out_specs=pl.BlockSpec((tm, D), lambda i: (i, 0)),
        out_shape=jax.ShapeDtypeStruct((T, D), F32),
        compiler_params=_cparams("parallel"),
        name="moe_combine",
    )(yg, yg, yg, yg, resid, mod)


def _moe(h2, resid, mod, ctx_rows, router_w, router_b, w_gu, b_gu, w_down, b_down):
    T, D = h2.shape
    e128, g128 = _router(h2, router_w, router_b)
    top_e, gates = e128[:, :TOP_K], g128[:, :TOP_K]
    onehot = (top_e[:, :, None] == jnp.arange(N_EXPERTS, dtype=jnp.int32)).astype(jnp.int32).sum(1)
    counts = onehot.sum(0)
    rank = jnp.cumsum(onehot, axis=0) - onehot
    pcounts = (counts + MOE_BLOCK - 1) // MOE_BLOCK * MOE_BLOCK
    pends = jnp.cumsum(pcounts)
    pstarts = pends - pcounts
    dest = (pstarts[top_e] + jnp.take_along_axis(rank, top_e, axis=1)).astype(jnp.int32)
    n_blocks = -(-T * TOP_K // MOE_BLOCK) + N_EXPERTS
    n_rows = n_blocks * MOE_BLOCK
    flat_dest = dest.reshape(-1)
    tok = jnp.repeat(jnp.arange(T, dtype=jnp.int32), TOP_K)
    row_tok = jnp.zeros((n_rows,), jnp.int32).at[flat_dest].set(tok)
    row_gate = jnp.zeros((n_rows,), F32).at[flat_dest].set(gates.reshape(-1))
    block_e = jnp.minimum(
        jnp.searchsorted(pends, jnp.arange(n_blocks, dtype=pends.dtype) * MOE_BLOCK, side='right'),
        N_EXPERTS - 1).astype(jnp.int32)
    n_used = (pends[-1] // MOE_BLOCK).astype(jnp.int32).reshape(1)
    xg = _gather_rows(h2, row_tok)
    yb = _experts(xg, row_gate.reshape(n_rows, 1), block_e, n_used, w_gu, b_gu, w_down, b_down)
    yg = _gather_rows(yb, dest.T.reshape(-1))
    return _combine(yg, resid, mod, ctx_rows)


def _rope_tables(n_ctx, S):
    half = MLA_ROPE // 2
    inv = ROPE_THETA ** (-jnp.arange(0, half, 2, dtype=F32) / half)
    rows = jnp.repeat(jnp.arange(S // GRID_W), GRID_W).astype(F32)[:, None] * inv
    cols = jnp.tile(jnp.arange(GRID_W), S // GRID_W).astype(F32)[:, None] * inv
    ang = jnp.concatenate([rows, rows, cols, cols], -1)
    cos = jnp.concatenate([jnp.ones((n_ctx, MLA_ROPE), F32), jnp.cos(ang)], 0)
    sin = jnp.concatenate([jnp.zeros((n_ctx, MLA_ROPE), F32), jnp.sin(ang)], 0)
    return cos, sin


def _rope(x, cos, sin):
    q = MLA_ROPE // 4

    def rot(p):
        return jnp.concatenate([-p[..., q:], p[..., :q]], -1)

    xr = jnp.concatenate([rot(x[..., :2 * q]), rot(x[..., 2 * q:])], -1)
    return x * cos[:, None, :] + xr * sin[:, None, :]


def _pad_k(a, k):
    return jnp.pad(a, ((0, 0), (0, k - a.shape[1])))


def _lora(a, w, name):
    kp = -(-a.shape[1] // LANES) * LANES
    return _matmul(_pad_k(a, kp).astype(BF16), jnp.pad(w, ((0, kp - w.shape[0]), (0, 0))).astype(BF16), name=name)


def _seg_shift(p, mu, n_ctx):
    def one(s):
        prev = jnp.pad(s, ((1, 0), (0, 0)))[:-1]
        nxt = jnp.pad(s, ((0, 1), (0, 0)))[1:]
        return s + mu[0] * (prev - s) + mu[1] * (nxt - s)

    return jnp.concatenate([one(p[:n_ctx]), one(p[n_ctx:])], 0)


def _rwkv_mixer(p_rw, lp, n_ctx):
    T = p_rw.shape[0]
    ps = _seg_shift(p_rw, lp['rwkv_mu'], n_ctx)
    o = 0
    parts = []
    for wdt in (W_B, W_B, W_B, R_W, R_W, R_A, R_A, R_G):
        parts.append(ps[:, o:o + wdt])
        o += wdt
    r, k, v, wd_f, wd_b, ad_f, ad_b, gd = parts
    heads = lambda t: t.reshape(T, H_B, N_B)
    kk = heads(k * lp['rwkv_k_k'])
    kk = (kk * lax.rsqrt(jnp.sum(kk * kk, -1, keepdims=True) + 1e-12)).reshape(T, W_B)
    ws, kds, bs = [], [], []
    for d, (wd, ad) in enumerate(((wd_f, ad_f), (wd_b, ad_b))):
        w_pre = lp['rwkv_w0'][d] + _lora(jnp.tanh(wd), lp['rwkv_w_up'][d], "rwkv_w_lora")
        decay = jnp.exp(-jnp.exp(-jax.nn.softplus(-w_pre) - 0.5))
        a = jax.nn.sigmoid(lp['rwkv_a0'][d] + _lora(ad, lp['rwkv_a_up'][d], "rwkv_a_lora"))
        kd = k * (1.0 + (a - 1.0) * lp['rwkv_k_a'])
        ws.append(decay)
        kds.append(kd)
        bs.append(kk * a)
    g = _lora(jax.nn.sigmoid(gd), lp['rwkv_g_up'], "rwkv_g_lora")
    o_f, o_b = _rwkv_scan(r, v, kk, ws, kds, bs, n_ctx=n_ctx)
    o = heads(o_f + o_b)
    mu = jnp.mean(o, -1, keepdims=True)
    var = jnp.mean(jnp.square(o - mu), -1, keepdims=True)
    on = ((o - mu) * lax.rsqrt(var + GN_EPS)).reshape(T, W_B) * lp['rwkv_ln_g'] + lp['rwkv_ln_b']
    r_k = lp['rwkv_r_k'].reshape(H_B, N_B)
    rh = heads(r)
    coef = (jnp.sum(rh * heads(kds[0]) * r_k, -1, keepdims=True)
            + jnp.sum(rh * heads(kds[1]) * r_k, -1, keepdims=True))
    bonus = (coef * heads(v)).reshape(T, W_B)
    return ((on + bonus) * g).astype(BF16)


def _in_layout(D):
    o = {}
    o['g'] = 0
    o['q'] = N_BRANCH * D
    o['kv'] = o['q'] + Q_LORA
    o['rw'] = o['kv'] + KV_LORA
    o['sq'] = o['rw'] + C_RWKV
    o['sk'] = o['sq'] + W_C
    o['sv'] = o['sk'] + KV_C * HD_C
    o['kr'] = o['sv'] + KV_C * HD_C
    return o


def _prep_w_in(w_in, D):
    s = np.cumsum([0, Q_LORA, KV_LORA, MLA_ROPE, C_RWKV, W_C, KV_C * HD_C, KV_C * HD_C, N_BRANCH * D])
    seg = lambda i: w_in[:, int(s[i]):int(s[i + 1])]
    pad = (-(int(s[-1])) % (2 * LANES))
    cols = [seg(7), seg(0), seg(1), seg(3), seg(4), seg(5), seg(6), seg(2)]
    if pad:
        cols.append(jnp.zeros((w_in.shape[0], pad), w_in.dtype))
    return jnp.concatenate(cols, -1).astype(BF16)


def _rot_cols(w):
    q = MLA_ROPE // 4
    return jnp.concatenate([-w[..., q:2 * q], w[..., :q], -w[..., 3 * q:], w[..., 2 * q:3 * q]], -1)


def _prep_wq_up(wq_up):
    w = wq_up.reshape(Q_LORA, H_A, MLA_NOPE + MLA_ROPE)
    w = jnp.concatenate([w, _rot_cols(w[..., MLA_NOPE:])], -1)
    return w.reshape(Q_LORA, H_A * MLA_QW).astype(BF16)


def _layer(xall, mod, lp, rope, n_ctx):
    T, D = xall.shape
    S = T - n_ctx
    cos, sin = rope
    md = [mod[:, j * D:(j + 1) * D] for j in range(6)]
    lo = _in_layout(D)

    h1 = _rownorm(xall, lp['norm1_g'], shift=md[0], scale=md[1], ctx_rows=n_ctx, name="norm1")
    w_in = _prep_w_in(lp['w_in'], D)
    p = _matmul(h1, w_in, tm=_tile(T, (1280, 640, 256, 128)), tn=_tile(w_in.shape[1], (896, 512, 256, 128)),
                name="w_in")

    qn = _rownorm(p, lp['mla_qn_g'], col_off=lo['q'], width=Q_LORA, name="mla_qnorm")
    q_tab = jnp.concatenate([jnp.ones((T, MLA_NOPE), F32), cos, sin], -1) * (MLA_SCALE * LOG2E)
    q = _matmul(qn, _prep_wq_up(lp['mla_wq_up']), row_table=q_tab, out_dtype=BF16, name="mla_q_up")
    kvn = _rownorm(p, lp['mla_kvn_g'], col_off=lo['kv'], width=KV_LORA, name="mla_kvnorm")
    kv = _matmul(kvn, lp['mla_wkv_up'].astype(BF16), out_dtype=BF16, name="mla_kv_up")
    kr = _rope(p[:, None, lo['kr']:lo['kr'] + MLA_ROPE], cos, sin)[:, 0]
    kr = jnp.concatenate([kr, kr], -1).astype(BF16)
    oa_x = _mla_attention(q[n_ctx:], kv, kr, n_q=S, n_keys=T, name="mla_latent")
    oa_c = _mla_attention(q[:n_ctx], kv, kr, n_q=n_ctx, n_keys=n_ctx, name="mla_context")
    o_a = jnp.concatenate([oa_c, oa_x], 0)

    o_b = _rwkv_mixer(p[:, lo['rw']:lo['rw'] + C_RWKV], lp, n_ctx)

    qc = _rope(p[:, lo['sq']:lo['sq'] + W_C].reshape(T, H_C, HD_C), cos, sin) * SWA_SCALE
    kc = _rope(p[:, lo['sk']:lo['sk'] + KV_C * HD_C].reshape(T, KV_C, HD_C), cos, sin)
    vc = p[:, lo['sv']:lo['sv'] + KV_C * HD_C].reshape(T, KV_C, HD_C)
    q3, k3, v3 = (jnp.swapaxes(t, 0, 1).astype(BF16) for t in (qc, kc, vc))
    sink_rows = jnp.repeat(lp['swa_sink'].astype(F32).reshape(KV_C, G_C), 128, axis=1).reshape(KV_C, G_C * 128, 1)
    oc_x = _swa_attention(q3, k3, v3, sink_rows, n_ctx=n_ctx, band=True, name="swa_latent")
    oc_c = _swa_attention(q3, k3, v3, sink_rows, n_ctx=n_ctx, band=False, name="swa_context")
    o_c = jnp.swapaxes(jnp.concatenate([oc_c, oc_x], 1), 0, 1).reshape(T, W_C)

    m = _merge(jnp.stack([o_a, o_b, o_c], 0), lp['w_branch'].astype(BF16), p, lo['g'])
    x1 = _matmul(m, lp['w_out'].astype(BF16), resid=xall, mod=md[2], ctx_rows=n_ctx, name="w_out")
    h2 = _rownorm(x1, lp['norm2_g'], shift=md[3], scale=md[4], ctx_rows=n_ctx, out_dtype=F32, name="norm2")
    return _moe(h2, x1, md[5], n_ctx, lp['router_w'], lp['router_b'], lp['exp_w_gu'].astype(BF16),
                lp['exp_b_gu'], lp['exp_w_down'].astype(BF16), lp['exp_b_down'])


def kernel(x, c, ctx, c_ctx, ada_w, ada_b, norm1_g, w_in, mla_qn_g, mla_kvn_g, mla_wq_up, mla_wkv_up, rwkv_mu, rwkv_w0, rwkv_w_up, rwkv_a0, rwkv_a_up, rwkv_g_up, rwkv_k_k, rwkv_k_a, rwkv_r_k, rwkv_ln_g, rwkv_ln_b, swa_sink, w_branch, w_out, norm2_g, router_w, router_b, exp_w_gu, exp_b_gu, exp_w_down, exp_b_down, final_g):
    B, S, D = x.shape
    n_ctx = ctx.shape[1]
    assert B == 1 and c.shape[0] == 1
    depth = ada_w.shape[0]
    rope = _rope_tables(n_ctx, S)
    xall = jnp.concatenate([ctx[0], x[0]], 0)
    cond = jax.nn.silu(jnp.concatenate([c, c_ctx[None], jnp.zeros((6, D), F32)], 0))
    for i in range(depth):
        lp = dict(norm1_g=norm1_g[i], w_in=w_in[i],
                  mla_qn_g=mla_qn_g[i], mla_kvn_g=mla_kvn_g[i], mla_wq_up=mla_wq_up[i], mla_wkv_up=mla_wkv_up[i],
                  rwkv_mu=rwkv_mu[i], rwkv_w0=rwkv_w0[i], rwkv_w_up=rwkv_w_up[i], rwkv_a0=rwkv_a0[i],
                  rwkv_a_up=rwkv_a_up[i], rwkv_g_up=rwkv_g_up[i], rwkv_k_k=rwkv_k_k[i], rwkv_k_a=rwkv_k_a[i],
                  rwkv_r_k=rwkv_r_k[i], rwkv_ln_g=rwkv_ln_g[i], rwkv_ln_b=rwkv_ln_b[i], swa_sink=swa_sink[i],
                  w_branch=w_branch[i], w_out=w_out[i], norm2_g=norm2_g[i], router_w=router_w[i],
                  router_b=router_b[i], exp_w_gu=exp_w_gu[i], exp_b_gu=exp_b_gu[i],
                  exp_w_down=exp_w_down[i], exp_b_down=exp_b_down[i])
        mod = _matmul(cond, ada_w[i], bias=ada_b[i], tm=8, tn=512, name="adaln")
        xall = _layer(xall, mod, lp, rope, n_ctx)
    out = _rownorm(xall, final_g, row_off=n_ctx, rows=S, out_dtype=F32, name="final_norm")
    return out.reshape(B, S, D)
```

```python
import functools

import numpy as np
import jax
import jax.numpy as jnp
from jax import lax
from jax.experimental import pallas as pl
from jax.experimental.pallas import tpu as pltpu

F32 = jnp.float32
BF16 = jnp.bfloat16

GRID_W = 64
ROPE_THETA = 10000.0
NORM_EPS = 1e-6
H_A, MLA_NOPE, MLA_ROPE, MLA_V = 16, 128, 64, 128
Q_LORA, KV_LORA = 1024, 512
MLA_SCALE = (MLA_NOPE + MLA_ROPE) ** -0.5
LOG2E = 1.4426950408889634
MLA_QW = 256
H_B, N_B = 32, 64
W_B = H_B * N_B
R_W, R_A, R_G = 96, 96, 256
GN_EPS = 64e-5
H_C, KV_C, HD_C = 32, 4, 64
G_C = H_C // KV_C
WINDOW = 128
W_C = H_C * HD_C
SWA_SCALE = HD_C ** -0.5
N_BRANCH = 3
N_EXPERTS, TOP_K, D_EXPERT = 32, 4, 512
SWIGLU_LIMIT, SWIGLU_ALPHA = 7.0, 1.702
MOE_BLOCK = 128
C_RWKV = 3 * W_B + 2 * R_W + 2 * R_A + R_G

LANES = 128
VMEM_LIMIT = 56 * 1024 * 1024


def _cparams(*sem):
    return pltpu.CompilerParams(dimension_semantics=sem, vmem_limit_bytes=VMEM_LIMIT)


def _lane_tile(x, k):
    return x if k == 1 else jnp.concatenate([x] * k, axis=1)


def _tile(n, prefs):
    for t in prefs:
        if n % t == 0:
            return t
    return n


def _mm_kernel(*refs, nk, has_bias, resid, has_tab, ctx_rows, tm, precision):
    it = iter(refs)
    a_ref, b_ref = next(it), next(it)
    bias_ref = next(it) if has_bias else None
    res_ref, mod_ref = (next(it), next(it)) if resid else (None, None)
    tab_ref = next(it) if has_tab else None
    o_ref = next(it)
    acc_ref = next(it) if nk > 1 else None

    if precision is None:
        part = jnp.dot(a_ref[...].astype(BF16), b_ref[...].astype(BF16), preferred_element_type=F32)
    else:
        part = jnp.dot(a_ref[...], b_ref[...], preferred_element_type=F32, precision=precision)

    def finish(acc):
        if has_bias:
            acc = acc + bias_ref[...]
        if resid:
            rows = pl.program_id(0) * tm + lax.broadcasted_iota(jnp.int32, (tm, 1), 0)
            m = jnp.where(rows < ctx_rows, mod_ref[1:2, :], mod_ref[0:1, :])
            acc = res_ref[...] + m * acc
        if has_tab:
            acc = acc * _lane_tile(tab_ref[...], acc.shape[1] // tab_ref.shape[1])
        o_ref[...] = acc.astype(o_ref.dtype)

    if nk == 1:
        finish(part)
    else:
        k = pl.program_id(2)

        @pl.when(k == 0)
        def _():
            acc_ref[...] = part

        @pl.when(k > 0)
        def _():
            acc_ref[...] += part

        @pl.when(k == nk - 1)
        def _():
            finish(acc_ref[...])


def _matmul(a, b, *, a_col_off=0, k_dim=None, bias=None, resid=None, mod=None, ctx_rows=0, row_table=None,
            out_dtype=F32, tm=None, tn=None, tk=None, precision=None, name="matmul"):
    M = a.shape[0]
    K, N = b.shape
    if k_dim is None:
        assert a.shape[1] == K and a_col_off == 0
    tm = tm or _tile(M, (640, 512, 256, 128))
    tn = tn or _tile(N, (1024, 896, 512, 256, 128))
    tk = tk or K
    assert M % tm == 0 and N % tn == 0 and K % tk == 0 and a_col_off % tk == 0
    nk = K // tk
    koff = a_col_off // tk
    in_specs = [pl.BlockSpec((tm, tk), lambda i, j, k: (i, k + koff)),
                pl.BlockSpec((tk, tn), lambda i, j, k: (k, j))]
    args = [a, b]
    if bias is not None:
        in_specs.append(pl.BlockSpec((1, tn), lambda i, j, k: (0, j)))
        args.append(bias.reshape(1, N).astype(F32))
    if resid is not None:
        in_specs.append(pl.BlockSpec((tm, tn), lambda i, j, k: (i, j)))
        in_specs.append(pl.BlockSpec((8, tn), lambda i, j, k: (0, j)))
        args += [resid, mod]
    if row_table is not None:
        assert tn % row_table.shape[1] == 0
        in_specs.append(pl.BlockSpec((tm, row_table.shape[1]), lambda i, j, k: (i, 0)))
        args.append(row_table)
    kern = functools.partial(_mm_kernel, nk=nk, has_bias=bias is not None, resid=resid is not None,
                             has_tab=row_table is not None, ctx_rows=ctx_rows, tm=tm, precision=precision)
    return pl.pallas_call(
        kern,
        grid=(M // tm, N // tn, nk),
        in_specs=in_specs,
        out_specs=pl.BlockSpec((tm, tn), lambda i, j, k: (i, j)),
        out_shape=jax.ShapeDtypeStruct((M, N), out_dtype),
        scratch_shapes=[pltpu.VMEM((tm, tn), F32)] if nk > 1 else [],
        compiler_params=_cparams("parallel", "parallel", "arbitrary"),
        name=name,
    )(*args)


def _norm_kernel(*refs, eps, modulated, ctx_rows, tm):
    if modulated:
        x_ref, g_ref, sh_ref, sc_ref, o_ref = refs
    else:
        x_ref, g_ref, o_ref = refs
    x = x_ref[...].astype(F32)
    y = x * lax.rsqrt(jnp.mean(x * x, axis=-1, keepdims=True) + eps) * g_ref[...]
    if modulated:
        rows = pl.program_id(0) * tm + lax.broadcasted_iota(jnp.int32, (tm, 1), 0)
        isctx = rows < ctx_rows
        sc = jnp.where(isctx, sc_ref[1:2, :], sc_ref[0:1, :])
        sh = jnp.where(isctx, sh_ref[1:2, :], sh_ref[0:1, :])
        y = y * (1.0 + sc) + sh
    o_ref[...] = y.astype(o_ref.dtype)


def _rownorm(x, g, *, col_off=0, width=None, row_off=0, rows=None, shift=None, scale=None,
             ctx_rows=0, out_dtype=BF16, eps=NORM_EPS, name="rownorm"):
    width = width or x.shape[1]
    rows = rows or x.shape[0]
    tm = _tile(rows, (256, 128))
    assert col_off % width == 0 and row_off % tm == 0 and rows % tm == 0
    cb, rb = col_off // width, row_off // tm
    modulated = shift is not None
    in_specs = [pl.BlockSpec((tm, width), lambda i: (i + rb, cb)),
                pl.BlockSpec((1, width), lambda i: (0, 0))]
    args = [x, g.reshape(1, width).astype(F32)]
    if modulated:
        in_specs += [pl.BlockSpec((8, width), lambda i: (0, 0))] * 2
        args += [shift, scale]
    kern = functools.partial(_norm_kernel, eps=eps, modulated=modulated, ctx_rows=ctx_rows, tm=tm)
    return pl.pallas_call(
        kern,
        grid=(rows // tm,),
        in_specs=in_specs,
        out_specs=pl.BlockSpec((tm, width), lambda i: (i, 0)),
        out_shape=jax.ShapeDtypeStruct((rows, width), out_dtype),
        compiler_params=_cparams("parallel"),
        name=name,
    )(*args)


def _flash_kernel(q_ref, kn_ref, kr_ref, v_ref, o_ref, m_ref, acc_ref, *, nk, sub):
    kj = pl.program_id(2)

    @pl.when(kj == 0)
    def _():
        m_ref[...] = jnp.full_like(m_ref, -jnp.inf)
        acc_ref[...] = jnp.zeros_like(acc_ref)

    q = q_ref[...]
    m, acc = m_ref[...], acc_ref[...]
    ones = jnp.ones((sub, MLA_V), BF16)
    for c in range(kn_ref.shape[0] // sub):
        rows = slice(c * sub, (c + 1) * sub)
        k = jnp.concatenate([kn_ref[rows, :], kr_ref[rows, :]], axis=-1)
        s = lax.dot_general(q, k, (((1,), (1,)), ((), ())), preferred_element_type=F32)
        m_new = jnp.maximum(m, jnp.max(s, axis=-1, keepdims=True))
        p = jnp.exp2(s - _lane_tile(m_new, sub // LANES)).astype(BF16)
        v1 = jnp.concatenate([v_ref[rows, :], ones], axis=-1)
        acc = _lane_tile(jnp.exp2(m - m_new), 2) * acc + jnp.dot(p, v1, preferred_element_type=F32)
        m = m_new
    m_ref[...] = m
    acc_ref[...] = acc

    @pl.when(kj == nk - 1)
    def _():
        acc = acc_ref[...]
        o_ref[...] = (acc[:, :MLA_V] / acc[:, MLA_V:]).astype(o_ref.dtype)


def _mla_attention(q, kv, kr, *, n_q, n_keys, name):
    tq = _tile(n_q, (1024, 512, 256, 128))
    tk = _tile(n_keys, (1280, 640, 512, 256, 128))
    nk = n_keys // tk
    sub = _tile(tk, (256, 128))
    return pl.pallas_call(
        functools.partial(_flash_kernel, nk=nk, sub=sub),
        grid=(H_A, n_q // tq, nk),
        in_specs=[pl.BlockSpec((tq, MLA_QW), lambda h, i, j: (i, h)),
                  pl.BlockSpec((tk, MLA_NOPE), lambda h, i, j: (j, 2 * h)),
                  pl.BlockSpec((tk, LANES), lambda h, i, j: (j, 0)),
                  pl.BlockSpec((tk, MLA_V), lambda h, i, j: (j, 2 * h + 1))],
        out_specs=pl.BlockSpec((tq, MLA_V), lambda h, i, j: (i, h)),
        out_shape=jax.ShapeDtypeStruct((n_q, H_A * MLA_V), BF16),
        scratch_shapes=[pltpu.VMEM((tq, LANES), F32), pltpu.VMEM((tq, 2 * MLA_V), F32)],
        compiler_params=_cparams("parallel", "parallel", "arbitrary"),
        name=name,
    )(q, kv, kr, kv)


def _swa_kernel(*refs, band, nb):
    if band:
        (q_ref, c_ref, s1_ref, s2_ref, sink_ref, kc_ref, vc_ref, kp_ref, k0_ref, kn_ref, vp_ref, v0_ref, vn_ref,
         o_ref) = refs
    else:
        q_ref, c_ref, s1_ref, s2_ref, sink_ref, kc_ref, vc_ref, o_ref = refs
    i = pl.program_id(0)
    qb = q_ref.shape[0]
    cos, s1, s2 = c_ref[...], s1_ref[...], s2_ref[...]

    def roped(x):
        return (x * cos + pltpu.roll(x, LANES - HD_C // 4, axis=1) * s1 + pltpu.roll(x, HD_C // 4, axis=1) * s2)

    lane = lax.broadcasted_iota(jnp.int32, (1, LANES), 1)
    low = lane < HD_C
    if band:
        r = lax.broadcasted_iota(jnp.int32, (qb, qb), 0)
        c = lax.broadcasted_iota(jnp.int32, (qb, qb), 1)
        ok_prev = c - r + jnp.where(i > 0, 0, -4 * qb) >= qb - WINDOW
        ok_next = r - c + jnp.where(i < nb - 1, 0, -4 * qb) >= qb - WINDOW

    def masked(s, ok):
        return jnp.where(ok[None], s.reshape(G_C, qb, qb), -jnp.inf).reshape(G_C * qb, qb)

    for g in range(KV_C):
        grp, upper = g // 2, g % 2 == 1
        mine = ~low if upper else low

        def both_halves(x_ref):
            x = x_ref[:, grp * LANES:(grp + 1) * LANES]
            return jnp.where(mine, x, pltpu.roll(x, HD_C, axis=1))

        qs = []
        for cgrp in range(G_C // 2):
            col = (g * (G_C // 2) + cgrp) * LANES
            x = roped(q_ref[:, col:col + LANES])
            qs += [jnp.where(low, x, 0.0), jnp.where(low, 0.0, x)]
        qs = jnp.concatenate(qs, axis=0).astype(BF16)

        def scores(k_ref):
            return lax.dot_general(qs, both_halves(k_ref).astype(BF16), (((1,), (1,)), ((), ())),
                                   preferred_element_type=F32)

        parts = [(scores(kc_ref), vc_ref)]
        if band:
            parts += [(masked(scores(kp_ref), ok_prev), vp_ref), (scores(k0_ref), v0_ref),
                      (masked(scores(kn_ref), ok_next), vn_ref)]
        sink = sink_ref[g]
        m = sink
        for s, _ in parts:
            m = jnp.maximum(m, jnp.max(s, axis=-1, keepdims=True))
        l = jnp.exp2(sink - m)
        acc_lo = jnp.zeros((G_C * qb, LANES), F32)
        acc_hi = jnp.zeros((G_C * qb, LANES), F32)
        for s, v_ref in parts:
            p = jnp.exp2(s - m)
            l = l + jnp.sum(p, axis=-1, keepdims=True)
            p = p.astype(BF16)
            v2 = both_halves(v_ref)
            acc_lo = acc_lo + jnp.dot(p, jnp.where(low, v2, 0).astype(BF16), preferred_element_type=F32)
            acc_hi = acc_hi + jnp.dot(p, jnp.where(low, 0, v2).astype(BF16), preferred_element_type=F32)
        inv = 1.0 / l
        for cgrp in range(G_C // 2):
            lo_rows = slice((2 * cgrp) * qb, (2 * cgrp + 1) * qb)
            hi_rows = slice((2 * cgrp + 1) * qb, (2 * cgrp + 2) * qb)
            tile = acc_lo[lo_rows] * inv[lo_rows] + acc_hi[hi_rows] * inv[hi_rows]
            col = (g * (G_C // 2) + cgrp) * LANES
            o_ref[:, col:col + LANES] = tile.astype(o_ref.dtype)


def _swa_attention(p, q_off, tabs, kr, vv, sink_rows, *, n_ctx, band, name):
    T = p.shape[0]
    qb = 128
    assert WINDOW == qb and n_ctx % qb == 0 and q_off % W_C == 0
    cb = n_ctx // qb
    kvw = KV_C * HD_C
    nb, row0 = ((T - n_ctx) // qb, cb) if band else (cb, 0)
    tab = pl.BlockSpec((qb, LANES), lambda i: (i + row0, 0))
    in_specs = [pl.BlockSpec((qb, W_C), lambda i: (i + row0, q_off // W_C)), tab, tab, tab,
                pl.BlockSpec((KV_C, G_C * qb, 1), lambda i: (0, 0, 0)),
                pl.BlockSpec((n_ctx, kvw), lambda i: (0, 0)),
                pl.BlockSpec((n_ctx, kvw), lambda i: (0, 0))]
    args = [p, *tabs, sink_rows, kr, vv]
    if band:
        blk = lambda off: pl.BlockSpec((qb, kvw), lambda i: (cb + jnp.clip(i + off, 0, nb - 1), 0))
        in_specs += [blk(-1), blk(0), blk(1)] * 2
        args += [kr, kr, kr, vv, vv, vv]
    return pl.pallas_call(
        functools.partial(_swa_kernel, band=band, nb=nb),
        grid=(nb,),
        in_specs=in_specs,
        out_specs=pl.BlockSpec((qb, W_C), lambda i: (i, 0)),
        out_shape=jax.ShapeDtypeStruct((nb * qb, W_C), BF16),
        compiler_params=_cparams("parallel"),
        name=name,
    )(*args)


N_PAIR = H_B // 2


def _scan_kernel(rf_ref, vf_ref, kkf_ref, wf_ref, kdf_ref, bf_ref, rb_ref, vb_ref, kkb_ref, wb_ref, kdb_ref,
                 bb_ref, of_ref, ob_ref, sf_ref, sb_ref, *, tb):
    @pl.when(pl.program_id(0) == 0)
    def _():
        sf_ref[...] = jnp.zeros_like(sf_ref)
        sb_ref[...] = jnp.zeros_like(sb_ref)

    lane_r = lax.broadcasted_iota(jnp.int32, (LANES, LANES), 0)
    lane_c = lax.broadcasted_iota(jnp.int32, (LANES, LANES), 1)
    seg = (lane_r // N_B == lane_c // N_B).astype(BF16)
    on_diag = (lax.broadcasted_iota(jnp.int32, (N_B, LANES), 0)
               == lax.broadcasted_iota(jnp.int32, (N_B, LANES), 1) % N_B)
    diag = on_diag.astype(F32)
    diag16 = on_diag.astype(BF16)

    def segsum(x):
        return jnp.dot(x, seg, preferred_element_type=F32)

    def tile_of(x, p):
        return x[p * N_B:(p + 1) * N_B, :]

    def one_step(i, tiles, s_ref):
        r_t, v_t, kk_t, w_t, kd_t, b_t = ([x[i:i + 1, :] for x in op] for op in tiles)
        sa = segsum(jnp.concatenate([(s_ref[p] * kk_t[p]).astype(BF16) for p in range(N_PAIR)], axis=0))
        vcol = segsum(jnp.concatenate(
            [diag16 * jnp.broadcast_to(v_t[p], (N_B, LANES)).astype(BF16) for p in range(N_PAIR)], axis=0))
        new = []
        for p in range(N_PAIR):
            s = s_ref[p] * w_t[p] - tile_of(sa, p) * b_t[p] + tile_of(vcol, p) * kd_t[p]
            s_ref[p] = s
            new.append((s * r_t[p]).astype(BF16))
        oc = segsum(jnp.concatenate(new, axis=0))
        return [jnp.sum(diag * tile_of(oc, p), axis=0, keepdims=True) for p in range(N_PAIR)]

    def load8(refs, row0):
        return [[ref[pl.ds(row0, 8), p * LANES:(p + 1) * LANES] for p in range(N_PAIR)] for ref in refs]

    def store8(o_ref, row0, rows):
        for p in range(N_PAIR):
            o_ref[pl.ds(row0, 8), p * LANES:(p + 1) * LANES] = jnp.concatenate([rows[i][p] for i in range(8)], 0)

    def group(g, carry):
        f0 = pl.multiple_of(g * 8, 8)
        b0 = pl.multiple_of(tb - 8 - g * 8, 8)
        f_tiles = load8((rf_ref, vf_ref, kkf_ref, wf_ref, kdf_ref, bf_ref), f0)
        b_tiles = load8((rb_ref, vb_ref, kkb_ref, wb_ref, kdb_ref, bb_ref), b0)
        f_rows, b_rows = [None] * 8, [None] * 8
        for i in range(8):
            f_rows[i] = one_step(i, f_tiles, sf_ref)
            b_rows[7 - i] = one_step(7 - i, b_tiles, sb_ref)
        store8(of_ref, f0, f_rows)
        store8(ob_ref, b0, b_rows)
        return carry

    lax.fori_loop(0, tb // 8, group, 0)


def _rwkv_scan(r, v, kk, ws, kds, bs, *, n_ctx):
    T = r.shape[0]
    tb = 128
    nb, cb = T // tb, n_ctx // tb
    assert T % tb == 0 and n_ctx % tb == 0

    def back(j):
        return jnp.where(j < cb, cb - 1 - j, nb - 1 - (j - cb))

    fwd = pl.BlockSpec((tb, W_B), lambda j: (j, 0))
    bwd = pl.BlockSpec((tb, W_B), lambda j: (back(j), 0))
    return pl.pallas_call(
        functools.partial(_scan_kernel, tb=tb),
        grid=(nb,),
        in_specs=[fwd] * 6 + [bwd] * 6,
        out_specs=[fwd, bwd],
        out_shape=[jax.ShapeDtypeStruct((T, W_B), F32)] * 2,
        scratch_shapes=[pltpu.VMEM((N_PAIR, N_B, LANES), F32)] * 2,
        compiler_params=_cparams("arbitrary"),
        name="rwkv_scan",
    )(r, v, kk, ws[0], kds[0], bs[0], r, v, kk, ws[1], kds[1], bs[1])


def _merge_kernel(o_ref, w_ref, g_ref, out_ref, acc_ref):
    j = pl.program_id(2)
    part = jax.nn.sigmoid(g_ref[...]) * jnp.dot(o_ref[0], w_ref[0], preferred_element_type=F32)

    @pl.when(j == 0)
    def _():
        acc_ref[...] = part

    @pl.when(j > 0)
    def _():
        acc_ref[...] += part

    @pl.when(j == N_BRANCH - 1)
    def _():
        out_ref[...] = acc_ref[...].astype(out_ref.dtype)


def _merge(o3, w_branch, p, gate_off):
    _, T, W = o3.shape
    D = w_branch.shape[2]
    tm = _tile(T, (640, 512, 256, 128))
    tn = _tile(D, (1024, 512, 256, 128))
    assert gate_off % tn == 0
    gb, nj = gate_off // tn, D // tn
    return pl.pallas_call(
        _merge_kernel,
        grid=(T // tm, D // tn, N_BRANCH),
        in_specs=[pl.BlockSpec((1, tm, W), lambda i, n, j: (j, i, 0)),
                  pl.BlockSpec((1, W, tn), lambda i, n, j: (j, 0, n)),
                  pl.BlockSpec((tm, tn), lambda i, n, j: (i, gb + j * nj + n))],
        out_specs=pl.BlockSpec((tm, tn), lambda i, n, j: (i, n)),
        out_shape=jax.ShapeDtypeStruct((T, D), BF16),
        scratch_shapes=[pltpu.VMEM((tm, tn), F32)],
        compiler_params=_cparams("parallel", "parallel", "arbitrary"),
        name="merge",
    )(o3, w_branch, p)


def _router_kernel(x_ref, w_ref, b_ref, e_ref, g_ref):
    logits = jnp.dot(x_ref[...], w_ref[...], preferred_element_type=F32,
                     precision=lax.Precision.HIGHEST) + b_ref[...]
    lane = lax.broadcasted_iota(jnp.int32, logits.shape, 1)
    vals, idxs = [], []
    x = logits
    for _ in range(TOP_K):
        m = jnp.max(x, axis=-1, keepdims=True)
        idx = jnp.min(jnp.where(x == m, lane, LANES), axis=-1, keepdims=True)
        vals.append(m)
        idxs.append(idx)
        x = jnp.where(lane == idx, -jnp.inf, x)
    ex = [jnp.exp(v - vals[0]) for v in vals]
    tot = ex[0] + ex[1] + ex[2] + ex[3]
    e_out = jnp.zeros(logits.shape, jnp.int32)
    g_out = jnp.zeros(logits.shape, F32)
    for k in range(TOP_K):
        e_out = jnp.where(lane == k, idxs[k], e_out)
        g_out = jnp.where(lane == k, ex[k] / tot, g_out)
    e_ref[...] = e_out
    g_ref[...] = g_out


def _router(h, router_w, router_b):
    T, D = h.shape
    tm = _tile(T, (256, 128))
    w = jnp.zeros((D, LANES), F32).at[:, :N_EXPERTS].set(router_w)
    b = jnp.full((1, LANES), -1e30, F32).at[0, :N_EXPERTS].set(router_b)
    return pl.pallas_call(
        _router_kernel,
        grid=(T // tm,),
        in_specs=[pl.BlockSpec((tm, D), lambda i: (i, 0)),
                  pl.BlockSpec((D, LANES), lambda i: (0, 0)),
                  pl.BlockSpec((1, LANES), lambda i: (0, 0))],
        out_specs=[pl.BlockSpec((tm, LANES), lambda i: (i, 0))] * 2,
        out_shape=[jax.ShapeDtypeStruct((T, LANES), jnp.int32), jax.ShapeDtypeStruct((T, LANES), F32)],
        compiler_params=_cparams("parallel"),
        name="router_topk",
    )(h, w, b)


GATHER_ROWS = 128


def _gather_kernel(idx_ref, src_ref, o_ref, sem):
    def row_copy(r):
        return pltpu.make_async_copy(src_ref.at[pl.ds(idx_ref[0, 0, r], 1), :], o_ref.at[pl.ds(r, 1), :], sem)

    def start(r, c):
        row_copy(r).start()
        return c

    def wait(r, c):
        row_copy(r).wait()
        return c

    lax.fori_loop(0, GATHER_ROWS, start, 0)
    lax.fori_loop(0, GATHER_ROWS, wait, 0)


def _gather_rows(src, idx):
    n = idx.shape[0]
    D = src.shape[1]
    assert n % GATHER_ROWS == 0
    nb = n // GATHER_ROWS
    return pl.pallas_call(
        _gather_kernel,
        grid=(nb,),
        in_specs=[pl.BlockSpec((1, 1, GATHER_ROWS), lambda i: (i, 0, 0), memory_space=pltpu.SMEM),
                  pl.BlockSpec(memory_space=pl.ANY)],
        out_specs=pl.BlockSpec((GATHER_ROWS, D), lambda i: (i, 0)),
        out_shape=jax.ShapeDtypeStruct((n, D), src.dtype),
        scratch_shapes=[pltpu.SemaphoreType.DMA(())],
        compiler_params=_cparams("arbitrary"),
        name="gather_rows",
    )(idx.reshape(nb, 1, GATHER_ROWS), src)


def _expert_kernel(be_ref, nu_ref, x_ref, g_ref, wgu_ref, bgu_ref, wd_ref, bd_ref, o_ref):
    i = pl.program_id(0)

    @pl.when(i < nu_ref[0])
    def _():
        gu = jnp.dot(x_ref[...].astype(BF16), wgu_ref[0], preferred_element_type=F32) + bgu_ref[0]
        gate = jnp.minimum(gu[:, :D_EXPERT], SWIGLU_LIMIT)
        up = jnp.clip(gu[:, D_EXPERT:], -SWIGLU_LIMIT, SWIGLU_LIMIT)
        act = (up + 1.0) * gate * jax.nn.sigmoid(gate * SWIGLU_ALPHA)
        y = jnp.dot(act.astype(BF16), wd_ref[0], preferred_element_type=F32) + bd_ref[0]
        o_ref[...] = y * g_ref[...]

    @pl.when(i >= nu_ref[0])
    def _():
        o_ref[...] = jnp.zeros_like(o_ref)


def _experts(xg, row_gate, block_e, n_used, w_gu, b_gu, w_down, b_down):
    n_rows, D = xg.shape
    nb = n_rows // MOE_BLOCK
    grid_spec = pltpu.PrefetchScalarGridSpec(
        num_scalar_prefetch=2,
        grid=(nb,),
        in_specs=[pl.BlockSpec((MOE_BLOCK, D), lambda i, be, nu: (i, 0)),
                  pl.BlockSpec((MOE_BLOCK, 1), lambda i, be, nu: (i, 0)),
                  pl.BlockSpec((1, D, 2 * D_EXPERT), lambda i, be, nu: (be[i], 0, 0)),
                  pl.BlockSpec((1, 1, 2 * D_EXPERT), lambda i, be, nu: (be[i], 0, 0)),
                  pl.BlockSpec((1, D_EXPERT, D), lambda i, be, nu: (be[i], 0, 0)),
                  pl.BlockSpec((1, 1, D), lambda i, be, nu: (be[i], 0, 0))],
        out_specs=pl.BlockSpec((MOE_BLOCK, D), lambda i, be, nu: (i, 0)),
    )
    return pl.pallas_call(
        _expert_kernel,
        grid_spec=grid_spec,
        out_shape=jax.ShapeDtypeStruct((n_rows, D), F32),
        compiler_params=_cparams("arbitrary"),
        name="experts",
    )(block_e, n_used, xg, row_gate, w_gu, b_gu.reshape(N_EXPERTS, 1, -1), w_down,
      b_down.reshape(N_EXPERTS, 1, -1))


def _combine_kernel(y0_ref, y1_ref, y2_ref, y3_ref, res_ref, mod_ref, o_ref, *, ctx_rows, tm):
    y = (y0_ref[...] + y1_ref[...]) + (y2_ref[...] + y3_ref[...])
    rows = pl.program_id(0) * tm + lax.broadcasted_iota(jnp.int32, (tm, 1), 0)
    m = jnp.where(rows < ctx_rows, mod_ref[1:2, :], mod_ref[0:1, :])
    o_ref[...] = res_ref[...] + m * y


def _combine(yg, resid, mod, ctx_rows):
    T, D = resid.shape
    tm = _tile(T, (128,))
    nt = T // tm
    assert TOP_K == 4
    return pl.pallas_call(
        functools.partial(_combine_kernel, ctx_rows=ctx_rows, tm=tm),
        grid=(nt,),
        in_specs=[pl.BlockSpec((tm, D), functools.partial(lambda i, k: (k * nt + i, 0), k=k)) for k in range(TOP_K)]
        + [pl.BlockSpec((tm, D), lambda i: (i, 0)), pl.BlockSpec((8, D), lambda i: (0, 0))],
        out_specs=pl.BlockSpec((tm, D), lambda i: (i, 0)),
        out_shape=jax.ShapeDtypeStruct((T, D), F32),
        compiler_params=_cparams("parallel"),
        name="moe_combine",
    )(yg, yg, yg, yg, resid, mod)


def _moe(h2, resid, mod, ctx_rows, router_w, router_b, w_gu, b_gu, w_down, b_down):
    T, D = h2.shape
    e128, g128 = _router(h2, router_w, router_b)
    top_e, gates = e128[:, :TOP_K], g128[:, :TOP_K]
    onehot = (top_e[:, :, None] == jnp.arange(N_EXPERTS, dtype=jnp.int32)).astype(jnp.int32).sum(1)
    counts = onehot.sum(0)
    rank = jnp.cumsum(onehot, axis=0) - onehot
    pcounts = (counts + MOE_BLOCK - 1) // MOE_BLOCK * MOE_BLOCK
    pends = jnp.cumsum(pcounts)
    pstarts = pends - pcounts
    dest = (pstarts[top_e] + jnp.take_along_axis(rank, top_e, axis=1)).astype(jnp.int32)
    n_blocks = -(-T * TOP_K // MOE_BLOCK) + N_EXPERTS
    n_rows = n_blocks * MOE_BLOCK
    flat_dest = dest.reshape(-1)
    tok = jnp.repeat(jnp.arange(T, dtype=jnp.int32), TOP_K)
    row_tok = jnp.zeros((n_rows,), jnp.int32).at[flat_dest].set(tok)
    row_gate = jnp.zeros((n_rows,), F32).at[flat_dest].set(gates.reshape(-1))
    block_e = jnp.minimum(
        jnp.searchsorted(pends, jnp.arange(n_blocks, dtype=pends.dtype) * MOE_BLOCK, side='right'),
        N_EXPERTS - 1).astype(jnp.int32)
    n_used = (pends[-1] // MOE_BLOCK).astype(jnp.int32).reshape(1)
    xg = _gather_rows(h2, row_tok)
    yb = _experts(xg, row_gate.reshape(n_rows, 1), block_e, n_used, w_gu, b_gu, w_down, b_down)
    yg = _gather_rows(yb, dest.T.reshape(-1))
    return _combine(yg, resid, mod, ctx_rows)


def _rope_tables(n_ctx, S):
    half = MLA_ROPE // 2
    inv = ROPE_THETA ** (-jnp.arange(0, half, 2, dtype=F32) / half)
    rows = jnp.repeat(jnp.arange(S // GRID_W), GRID_W).astype(F32)[:, None] * inv
    cols = jnp.tile(jnp.arange(GRID_W), S // GRID_W).astype(F32)[:, None] * inv
    ang = jnp.concatenate([rows, rows, cols, cols], -1)
    cos = jnp.concatenate([jnp.ones((n_ctx, MLA_ROPE), F32), jnp.cos(ang)], 0)
    sin = jnp.concatenate([jnp.zeros((n_ctx, MLA_ROPE), F32), jnp.sin(ang)], 0)
    first = (jnp.arange(LANES) % (MLA_ROPE // 2)) < MLA_ROPE // 4
    cos2, sin2 = (jnp.concatenate([t, t], -1) * (SWA_SCALE * LOG2E) for t in (cos, sin))
    swa_tabs = (cos2, jnp.where(first, -sin2, 0.0), jnp.where(first, 0.0, sin2))
    return cos, sin, swa_tabs


def _rope(x, cos, sin):
    q = MLA_ROPE // 4

    def rot(p):
        return jnp.concatenate([-p[..., q:], p[..., :q]], -1)

    xr = jnp.concatenate([rot(x[..., :2 * q]), rot(x[..., 2 * q:])], -1)
    return x * cos[:, None, :] + xr * sin[:, None, :]


def _pad_k(a, k):
    return jnp.pad(a, ((0, 0), (0, k - a.shape[1])))


LORA_W = 1024
O_WD_B, O_AD_F, O_AD_B, O_GD = R_W, 2 * R_W, 2 * R_W + R_A, 2 * R_W + 2 * R_A


def _seg_matrix():
    i = lax.broadcasted_iota(jnp.int32, (LANES, LANES), 0) // N_B
    j = lax.broadcasted_iota(jnp.int32, (LANES, LANES), 1) // N_B
    return (i == j).astype(BF16)


def _head_sums(x, seg):
    outs = []
    for c in range(x.shape[1] // LANES):
        rem = x[:, c * LANES:(c + 1) * LANES]
        tot = None
        for _ in range(3):
            hi = rem.astype(BF16)
            y = jnp.dot(hi, seg, preferred_element_type=F32)
            tot = y if tot is None else tot + y
            rem = rem - hi.astype(F32)
        outs.append(tot)
    return jnp.concatenate(outs, axis=1)


def _rwkv_prep_kernel(pr_ref, pk_ref, pv_ref, pl_ref, hp_r, hp_k, hp_v, hp_l, hn_r, hn_k, hn_v, hn_l,
                      mu_r, mu_k, mu_v, mu_l, kk_g, ka_g, w0_ref, a0_ref, wwf, wwb, waf, wab, wg,
                      r_o, v_o, kk_o, wf_o, kdf_o, bf_o, wb_o, kdb_o, bb_o, g_o, *, tm, n_ctx, n_tok):
    row = pl.program_id(0) * tm + lax.broadcasted_iota(jnp.int32, (tm, 1), 0)
    local = lax.broadcasted_iota(jnp.int32, (tm, 1), 0)
    has_prev = (row != 0) & (row != n_ctx)
    has_next = (row != n_ctx - 1) & (row != n_tok - 1)

    def shifted(x_ref, hp_ref, hn_ref, mu_ref):
        x = x_ref[...]
        prev = jnp.where(local == 0, hp_ref[7:8, :], pltpu.roll(x, 1, axis=0))
        nxt = jnp.where(local == tm - 1, hn_ref[0:1, :], pltpu.roll(x, tm - 1, axis=0))
        prev = jnp.where(has_prev, prev, 0.0)
        nxt = jnp.where(has_next, nxt, 0.0)
        return x + mu_ref[0:1, :] * (prev - x) + mu_ref[1:2, :] * (nxt - x)

    r = shifted(pr_ref, hp_r, hn_r, mu_r)
    k = shifted(pk_ref, hp_k, hn_k, mu_k)
    v = shifted(pv_ref, hp_v, hn_v, mu_v)
    lx = shifted(pl_ref, hp_l, hn_l, mu_l)
    seg = _seg_matrix()
    kk = k * kk_g[...]
    kk = kk * lax.rsqrt(_head_sums(kk * kk, seg) + 1e-12)
    r_o[...] = r
    v_o[...] = v
    kk_o[...] = kk
    th = jnp.tanh(lx[:, 0:2 * LANES]).astype(BF16)
    lin = lx[:, LANES:3 * LANES].astype(BF16)
    w_pre = (jnp.dot(th[:, 0:LANES], wwf[...], preferred_element_type=F32),
             jnp.dot(th, wwb[...], preferred_element_type=F32))
    a_pre = (jnp.dot(lin, waf[...], preferred_element_type=F32),
             jnp.dot(lin[:, LANES:], wab[...], preferred_element_type=F32))
    for d, (w_o, kd_o, b_o) in enumerate(((wf_o, kdf_o, bf_o), (wb_o, kdb_o, bb_o))):
        wp = w0_ref[d:d + 1, :] + w_pre[d]
        w_o[...] = jnp.exp(-jnp.exp(-jax.nn.softplus(-wp) - 0.5))
        a = jax.nn.sigmoid(a0_ref[d:d + 1, :] + a_pre[d])
        kd_o[...] = k * (1.0 + (a - 1.0) * ka_g[...])
        b_o[...] = kk * a
    gd = jax.nn.sigmoid(lx[:, O_GD:O_GD + R_G]).astype(BF16)
    g_o[...] = jnp.dot(gd, wg[...], preferred_element_type=F32)


def _rows8(a):
    return jnp.pad(a, ((0, 8 - a.shape[0]), (0, 0))).astype(F32)


def _rwkv_prep(p, lo, lp, n_ctx):
    T = p.shape[0]
    tm = 128
    assert T % tm == 0 and n_ctx % tm == 0
    hb = tm // 8
    last8 = T // 8 - 1
    main = lambda w, off: pl.BlockSpec((tm, w), lambda i: (i, off // w))
    halo_p = lambda w, off: pl.BlockSpec((8, w), lambda i: (jnp.maximum(i * hb - 1, 0), off // w))
    halo_n = lambda w, off: pl.BlockSpec((8, w), lambda i: (jnp.minimum((i + 1) * hb, last8), off // w))
    groups = ((W_B, lo['r']), (W_B, lo['k']), (W_B, lo['v']), (LORA_W, lo['lora']))
    for w, off in groups:
        assert off % w == 0
    full = lambda a: pl.BlockSpec(a.shape, lambda i: (0, 0))
    mu = lp['rwkv_mu']
    mus = [_rows8(mu[:, 0:W_B]), _rows8(mu[:, W_B:2 * W_B]), _rows8(mu[:, 2 * W_B:3 * W_B]),
           _rows8(jnp.pad(mu[:, 3 * W_B:], ((0, 0), (0, LORA_W - (C_RWKV - 3 * W_B)))))]

    def slot(w, lo_row, n_rows):
        return jnp.pad(w, ((lo_row, n_rows - lo_row - w.shape[0]), (0, 0))).astype(BF16)

    params = mus + [lp['rwkv_k_k'].reshape(1, W_B), lp['rwkv_k_a'].reshape(1, W_B),
                    _rows8(lp['rwkv_w0']), _rows8(lp['rwkv_a0']),
                    slot(lp['rwkv_w_up'][0], 0, LANES), slot(lp['rwkv_w_up'][1], O_WD_B, 2 * LANES),
                    slot(lp['rwkv_a_up'][0], O_AD_F - LANES, 2 * LANES),
                    slot(lp['rwkv_a_up'][1], O_AD_B - 2 * LANES, LANES),
                    lp['rwkv_g_up'].astype(BF16)]
    in_specs = ([main(w, off) for w, off in groups] + [halo_p(w, off) for w, off in groups]
                + [halo_n(w, off) for w, off in groups] + [full(a) for a in params])
    outs = pl.pallas_call(
        functools.partial(_rwkv_prep_kernel, tm=tm, n_ctx=n_ctx, n_tok=T),
        grid=(T // tm,),
        in_specs=in_specs,
        out_specs=[pl.BlockSpec((tm, W_B), lambda i: (i, 0))] * 10,
        out_shape=[jax.ShapeDtypeStruct((T, W_B), F32)] * 10,
        compiler_params=_cparams("parallel"),
        name="rwkv_prep",
    )(*([p] * 12 + params))
    return outs


def _rwkv_out_kernel(of_ref, ob_ref, r_ref, v_ref, kdf_ref, kdb_ref, g_ref, lng_ref, lnb_ref, rk_ref, o_ref):
    seg = _seg_matrix()
    o = of_ref[...] + ob_ref[...]
    d = o - _head_sums(o, seg) * (1.0 / N_B)
    var = _head_sums(d * d, seg) * (1.0 / N_B)
    on = d * lax.rsqrt(var + GN_EPS) * lng_ref[...] + lnb_ref[...]
    coef = _head_sums(r_ref[...] * (kdf_ref[...] + kdb_ref[...]) * rk_ref[...], seg)
    o_ref[...] = ((on + coef * v_ref[...]) * g_ref[...]).astype(o_ref.dtype)


def _rwkv_out(o_f, o_b, r, v, kd_f, kd_b, g, lp):
    T = r.shape[0]
    tm = _tile(T, (128,))
    blk = pl.BlockSpec((tm, W_B), lambda i: (i, 0))
    par = pl.BlockSpec((1, W_B), lambda i: (0, 0))
    return pl.pallas_call(
        _rwkv_out_kernel,
        grid=(T // tm,),
        in_specs=[blk] * 7 + [par] * 3,
        out_specs=blk,
        out_shape=jax.ShapeDtypeStruct((T, W_B), BF16),
        compiler_params=_cparams("parallel"),
        name="rwkv_out",
    )(o_f, o_b, r, v, kd_f, kd_b, g, lp['rwkv_ln_g'].reshape(1, W_B), lp['rwkv_ln_b'].reshape(1, W_B),
      lp['rwkv_r_k'].reshape(1, W_B))


def _rwkv_mixer(p, lo, lp, n_ctx):
    r, v, kk, w_f, kd_f, b_f, w_b, kd_b, b_b, g = _rwkv_prep(p, lo, lp, n_ctx)
    o_f, o_b = _rwkv_scan(r, v, kk, (w_f, w_b), (kd_f, kd_b), (b_f, b_b), n_ctx=n_ctx)
    return _rwkv_out(o_f, o_b, r, v, kd_f, kd_b, g, lp)


def _in_layout(D):
    o = {}
    o['g'] = 0
    o['r'] = N_BRANCH * D
    o['k'] = o['r'] + W_B
    o['v'] = o['k'] + W_B
    o['sq'] = o['v'] + W_B
    o['q'] = o['sq'] + W_C
    o['lora'] = o['q'] + Q_LORA
    o['kv'] = o['lora'] + LORA_W
    o['sk'] = o['kv'] + KV_LORA
    o['sv'] = o['sk'] + KV_C * HD_C
    o['kr'] = o['sv'] + KV_C * HD_C
    o['end'] = -(-(o['kr'] + MLA_ROPE) // 1024) * 1024
    return o


def _prep_w_in(w_in, D):
    s = [int(v) for v in np.cumsum([0, Q_LORA, KV_LORA, MLA_ROPE, C_RWKV, W_C, KV_C * HD_C, KV_C * HD_C,
                                    N_BRANCH * D])]
    seg = lambda i: w_in[:, s[i]:s[i + 1]]
    rw = seg(3)
    zeros = lambda n: jnp.zeros((w_in.shape[0], n), w_in.dtype)
    lo = _in_layout(D)
    cols = [seg(7), rw[:, :3 * W_B], seg(4), seg(0), rw[:, 3 * W_B:], zeros(LORA_W - (C_RWKV - 3 * W_B)), seg(1),
            seg(5), seg(6), seg(2), zeros(lo['end'] - lo['kr'] - MLA_ROPE)]
    return jnp.concatenate(cols, -1).astype(BF16)


def _rot_cols(w):
    q = MLA_ROPE // 4
    return jnp.concatenate([-w[..., q:2 * q], w[..., :q], -w[..., 3 * q:], w[..., 2 * q:3 * q]], -1)


def _prep_wq_up(wq_up):
    w = wq_up.reshape(Q_LORA, H_A, MLA_NOPE + MLA_ROPE)
    w = jnp.concatenate([w, _rot_cols(w[..., MLA_NOPE:])], -1)
    return w.reshape(Q_LORA, H_A * MLA_QW).astype(BF16)


def _layer(xall, mod, lp, rope, n_ctx):
    T, D = xall.shape
    S = T - n_ctx
    cos, sin, swa_tabs = rope
    md = [mod[:, j * D:(j + 1) * D] for j in range(6)]
    lo = _in_layout(D)

    h1 = _rownorm(xall, lp['norm1_g'], shift=md[0], scale=md[1], ctx_rows=n_ctx, name="norm1")
    w_in = _prep_w_in(lp['w_in'], D)
    p = _matmul(h1, w_in, tm=_tile(T, (1280, 640, 256, 128)), tn=512, name="w_in")

    qn = _rownorm(p, lp['mla_qn_g'], col_off=lo['q'], width=Q_LORA, name="mla_qnorm")
    q_tab = jnp.concatenate([jnp.ones((T, MLA_NOPE), F32), cos, sin], -1) * (MLA_SCALE * LOG2E)
    q = _matmul(qn, _prep_wq_up(lp['mla_wq_up']), row_table=q_tab, out_dtype=BF16, name="mla_q_up")
    kvn = _rownorm(p, lp['mla_kvn_g'], col_off=lo['kv'], width=KV_LORA, name="mla_kvnorm")
    kv = _matmul(kvn, lp['mla_wkv_up'].astype(BF16), out_dtype=BF16, name="mla_kv_up")
    kr = _rope(p[:, None, lo['kr']:lo['kr'] + MLA_ROPE], cos, sin)[:, 0]
    kr = jnp.concatenate([kr, kr], -1).astype(BF16)
    oa_x = _mla_attention(q[n_ctx:], kv, kr, n_q=S, n_keys=T, name="mla_latent")
    oa_c = _mla_attention(q[:n_ctx], kv, kr, n_q=n_ctx, n_keys=n_ctx, name="mla_context")
    o_a = jnp.concatenate([oa_c, oa_x], 0)

    o_b = _rwkv_mixer(p, lo, lp, n_ctx)

    kvw = KV_C * HD_C
    kc = _rope(p[:, lo['sk']:lo['sk'] + kvw].reshape(T, KV_C, HD_C), cos, sin).reshape(T, kvw)
    vc = p[:, lo['sv']:lo['sv'] + kvw]
    sink_rows = jnp.repeat(lp['swa_sink'].astype(F32).reshape(KV_C, G_C) * LOG2E, 128, axis=1)
    sink_rows = sink_rows.reshape(KV_C, G_C * 128, 1)
    oc_x = _swa_attention(p, lo['sq'], swa_tabs, kc, vc, sink_rows, n_ctx=n_ctx, band=True, name="swa_latent")
    oc_c = _swa_attention(p, lo['sq'], swa_tabs, kc, vc, sink_rows, n_ctx=n_ctx, band=False, name="swa_context")
    o_c = jnp.concatenate([oc_c, oc_x], 0)

    m = _merge(jnp.stack([o_a, o_b, o_c], 0), lp['w_branch'].astype(BF16), p, lo['g'])
    x1 = _matmul(m, lp['w_out'].astype(BF16), resid=xall, mod=md[2], ctx_rows=n_ctx, name="w_out")
    h2 = _rownorm(x1, lp['norm2_g'], shift=md[3], scale=md[4], ctx_rows=n_ctx, out_dtype=F32, name="norm2")
    return _moe(h2, x1, md[5], n_ctx, lp['router_w'], lp['router_b'], lp['exp_w_gu'].astype(BF16),
                lp['exp_b_gu'], lp['exp_w_down'].astype(BF16), lp['exp_b_down'])


def kernel(x, c, ctx, c_ctx, ada_w, ada_b, norm1_g, w_in, mla_qn_g, mla_kvn_g, mla_wq_up, mla_wkv_up, rwkv_mu, rwkv_w0, rwkv_w_up, rwkv_a0, rwkv_a_up, rwkv_g_up, rwkv_k_k, rwkv_k_a, rwkv_r_k, rwkv_ln_g, rwkv_ln_b, swa_sink, w_branch, w_out, norm2_g, router_w, router_b, exp_w_gu, exp_b_gu, exp_w_down, exp_b_down, final_g):
    B, S, D = x.shape
    n_ctx = ctx.shape[1]
    assert B == 1 and c.shape[0] == 1
    depth = ada_w.shape[0]
    rope = _rope_tables(n_ctx, S)
    xall = jnp.concatenate([ctx[0], x[0]], 0)
    cond = jax.nn.silu(jnp.concatenate([c, c_ctx[None], jnp.zeros((6, D), F32)], 0))
    for i in range(depth):
        lp = dict(norm1_g=norm1_g[i], w_in=w_in[i],
                  mla_qn_g=mla_qn_g[i], mla_kvn_g=mla_kvn_g[i], mla_wq_up=mla_wq_up[i], mla_wkv_up=mla_wkv_up[i],
                  rwkv_mu=rwkv_mu[i], rwkv_w0=rwkv_w0[i], rwkv_w_up=rwkv_w_up[i], rwkv_a0=rwkv_a0[i],
                  rwkv_a_up=rwkv_a_up[i], rwkv_g_up=rwkv_g_up[i], rwkv_k_k=rwkv_k_k[i], rwkv_k_a=rwkv_k_a[i],
                  rwkv_r_k=rwkv_r_k[i], rwkv_ln_g=rwkv_ln_g[i], rwkv_ln_b=rwkv_ln_b[i], swa_sink=swa_sink[i],
                  w_branch=w_branch[i], w_out=w_out[i], norm2_g=norm2_g[i], router_w=router_w[i],
                  router_b=router_b[i], exp_w_gu=exp_w_gu[i], exp_b_gu=exp_b_gu[i],
                  exp_w_down=exp_w_down[i], exp_b_down=exp_b_down[i])
        mod = _matmul(cond, ada_w[i], bias=ada_b[i], tm=8, tn=512, name="adaln")
        xall = _layer(xall, mod, lp, rope, n_ctx)
    out = _rownorm(xall, final_g, row_off=n_ctx, rows=S, out_dtype=F32, name="final_norm")
    return out.reshape(B, S, D)
```

```python
import functools

import numpy as np
import jax
import jax.numpy as jnp
from jax import lax
from jax.experimental import pallas as pl
from jax.experimental.pallas import tpu as pltpu

F32 = jnp.float32
BF16 = jnp.bfloat16

GRID_W = 64
ROPE_THETA = 10000.0
NORM_EPS = 1e-6
H_A, MLA_NOPE, MLA_ROPE, MLA_V = 16, 128, 64, 128
Q_LORA, KV_LORA = 1024, 512
MLA_SCALE = (MLA_NOPE + MLA_ROPE) ** -0.5
LOG2E = 1.4426950408889634
MLA_QW = 256
H_B, N_B = 32, 64
W_B = H_B * N_B
R_W, R_A, R_G = 96, 96, 256
GN_EPS = 64e-5
H_C, KV_C, HD_C = 32, 4, 64
G_C = H_C // KV_C
WINDOW = 128
W_C = H_C * HD_C
SWA_SCALE = HD_C ** -0.5
N_BRANCH = 3
N_EXPERTS, TOP_K, D_EXPERT = 32, 4, 512
SWIGLU_LIMIT, SWIGLU_ALPHA = 7.0, 1.702
MOE_BLOCK = 128
C_RWKV = 3 * W_B + 2 * R_W + 2 * R_A + R_G

LANES = 128
VMEM_LIMIT = 56 * 1024 * 1024


def _cparams(*sem):
    return pltpu.CompilerParams(dimension_semantics=sem, vmem_limit_bytes=VMEM_LIMIT)


def _lane_tile(x, k):
    return x if k == 1 else jnp.concatenate([x] * k, axis=1)


def _tile(n, prefs):
    for t in prefs:
        if n % t == 0:
            return t
    return n


def _mm_kernel(*refs, nk, has_bias, resid, has_tab, ctx_rows, tm, precision):
    it = iter(refs)
    a_ref, b_ref = next(it), next(it)
    bias_ref = next(it) if has_bias else None
    res_ref, mod_ref = (next(it), next(it)) if resid else (None, None)
    tab_ref = next(it) if has_tab else None
    o_ref = next(it)
    acc_ref = next(it) if nk > 1 else None

    if precision is None:
        part = jnp.dot(a_ref[...].astype(BF16), b_ref[...].astype(BF16), preferred_element_type=F32)
    else:
        part = jnp.dot(a_ref[...], b_ref[...], preferred_element_type=F32, precision=precision)

    def finish(acc):
        if has_bias:
            acc = acc + bias_ref[...]
        if resid:
            rows = pl.program_id(0) * tm + lax.broadcasted_iota(jnp.int32, (tm, 1), 0)
            m = jnp.where(rows < ctx_rows, mod_ref[1:2, :], mod_ref[0:1, :])
            acc = res_ref[...] + m * acc
        if has_tab:
            acc = acc * _lane_tile(tab_ref[...], acc.shape[1] // tab_ref.shape[1])
        o_ref[...] = acc.astype(o_ref.dtype)

    if nk == 1:
        finish(part)
    else:
        k = pl.program_id(2)

        @pl.when(k == 0)
        def _():
            acc_ref[...] = part

        @pl.when(k > 0)
        def _():
            acc_ref[...] += part

        @pl.when(k == nk - 1)
        def _():
            finish(acc_ref[...])


def _matmul(a, b, *, a_col_off=0, k_dim=None, bias=None, resid=None, mod=None, ctx_rows=0, row_table=None,
            b_slab=0, out_dtype=F32, tm=None, tn=None, tk=None, precision=None, name="matmul"):
    M = a.shape[0]
    K, N = b.shape[-2:]
    if k_dim is None:
        assert a.shape[1] == K and a_col_off == 0
    tm = tm or _tile(M, (640, 512, 256, 128))
    tn = tn or _tile(N, (1024, 896, 512, 256, 128))
    tk = tk or K
    assert M % tm == 0 and N % tn == 0 and K % tk == 0 and a_col_off % tk == 0
    nk = K // tk
    koff = a_col_off // tk
    if b.ndim == 3:
        b_spec = pl.BlockSpec((None, tk, tn), lambda i, j, k: (b_slab, k, j))
    else:
        b_spec = pl.BlockSpec((tk, tn), lambda i, j, k: (k, j))
    in_specs = [pl.BlockSpec((tm, tk), lambda i, j, k: (i, k + koff)), b_spec]
    args = [a, b]
    if bias is not None:
        in_specs.append(pl.BlockSpec((1, tn), lambda i, j, k: (0, j)))
        args.append(bias.reshape(1, N).astype(F32))
    if resid is not None:
        in_specs.append(pl.BlockSpec((tm, tn), lambda i, j, k: (i, j)))
        in_specs.append(pl.BlockSpec((8, tn), lambda i, j, k: (0, j)))
        args += [resid, mod]
    if row_table is not None:
        assert tn % row_table.shape[1] == 0
        in_specs.append(pl.BlockSpec((tm, row_table.shape[1]), lambda i, j, k: (i, 0)))
        args.append(row_table)
    kern = functools.partial(_mm_kernel, nk=nk, has_bias=bias is not None, resid=resid is not None,
                             has_tab=row_table is not None, ctx_rows=ctx_rows, tm=tm, precision=precision)
    return pl.pallas_call(
        kern,
        grid=(M // tm, N // tn, nk),
        in_specs=in_specs,
        out_specs=pl.BlockSpec((tm, tn), lambda i, j, k: (i, j)),
        out_shape=jax.ShapeDtypeStruct((M, N), out_dtype),
        scratch_shapes=[pltpu.VMEM((tm, tn), F32)] if nk > 1 else [],
        compiler_params=_cparams("parallel", "parallel", "arbitrary"),
        name=name,
    )(*args)


def _norm_kernel(*refs, eps, modulated, ctx_rows, tm):
    if modulated:
        x_ref, g_ref, sh_ref, sc_ref, o_ref = refs
    else:
        x_ref, g_ref, o_ref = refs
    x = x_ref[...].astype(F32)
    y = x * lax.rsqrt(jnp.mean(x * x, axis=-1, keepdims=True) + eps) * g_ref[...]
    if modulated:
        rows = pl.program_id(0) * tm + lax.broadcasted_iota(jnp.int32, (tm, 1), 0)
        isctx = rows < ctx_rows
        sc = jnp.where(isctx, sc_ref[1:2, :], sc_ref[0:1, :])
        sh = jnp.where(isctx, sh_ref[1:2, :], sh_ref[0:1, :])
        y = y * (1.0 + sc) + sh
    o_ref[...] = y.astype(o_ref.dtype)


def _rownorm(x, g, *, col_off=0, width=None, row_off=0, rows=None, shift=None, scale=None,
             ctx_rows=0, out_dtype=BF16, eps=NORM_EPS, name="rownorm"):
    width = width or x.shape[1]
    rows = rows or x.shape[0]
    tm = _tile(rows, (256, 128))
    assert col_off % width == 0 and row_off % tm == 0 and rows % tm == 0
    cb, rb = col_off // width, row_off // tm
    modulated = shift is not None
    in_specs = [pl.BlockSpec((tm, width), lambda i: (i + rb, cb)),
                pl.BlockSpec((1, width), lambda i: (0, 0))]
    args = [x, g.reshape(1, width).astype(F32)]
    if modulated:
        in_specs += [pl.BlockSpec((8, width), lambda i: (0, 0))] * 2
        args += [shift, scale]
    kern = functools.partial(_norm_kernel, eps=eps, modulated=modulated, ctx_rows=ctx_rows, tm=tm)
    return pl.pallas_call(
        kern,
        grid=(rows // tm,),
        in_specs=in_specs,
        out_specs=pl.BlockSpec((tm, width), lambda i: (i, 0)),
        out_shape=jax.ShapeDtypeStruct((rows, width), out_dtype),
        compiler_params=_cparams("parallel"),
        name=name,
    )(*args)


def _flash_kernel(q_ref, kn_ref, kr_ref, v_ref, o_ref, m_ref, acc_ref, *, nk, sub):
    kj = pl.program_id(2)

    @pl.when(kj == 0)
    def _():
        m_ref[...] = jnp.full_like(m_ref, -jnp.inf)
        acc_ref[...] = jnp.zeros_like(acc_ref)

    q = q_ref[...]
    m, acc = m_ref[...], acc_ref[...]
    ones = jnp.ones((sub, MLA_V), BF16)
    for c in range(kn_ref.shape[0] // sub):
        rows = slice(c * sub, (c + 1) * sub)
        k = jnp.concatenate([kn_ref[rows, :], kr_ref[rows, :]], axis=-1)
        s = lax.dot_general(q, k, (((1,), (1,)), ((), ())), preferred_element_type=F32)
        m_new = jnp.maximum(m, jnp.max(s, axis=-1, keepdims=True))
        p = jnp.exp2(s - _lane_tile(m_new, sub // LANES)).astype(BF16)
        v1 = jnp.concatenate([v_ref[rows, :], ones], axis=-1)
        acc = _lane_tile(jnp.exp2(m - m_new), 2) * acc + jnp.dot(p, v1, preferred_element_type=F32)
        m = m_new
    m_ref[...] = m
    acc_ref[...] = acc

    @pl.when(kj == nk - 1)
    def _():
        acc = acc_ref[...]
        o_ref[...] = (acc[:, :MLA_V] / acc[:, MLA_V:]).astype(o_ref.dtype)


def _mla_attention(q, kv, kr, *, n_q, n_keys, name):
    tq = _tile(n_q, (2048, 1024, 512, 256, 128))
    tk = _tile(n_keys, (1280, 640, 512, 256, 128))
    nk = n_keys // tk
    sub = _tile(tk, (256, 128))
    return pl.pallas_call(
        functools.partial(_flash_kernel, nk=nk, sub=sub),
        grid=(H_A, n_q // tq, nk),
        in_specs=[pl.BlockSpec((tq, MLA_QW), lambda h, i, j: (i, h)),
                  pl.BlockSpec((tk, MLA_NOPE), lambda h, i, j: (j, 2 * h)),
                  pl.BlockSpec((tk, LANES), lambda h, i, j: (j, 0)),
                  pl.BlockSpec((tk, MLA_V), lambda h, i, j: (j, 2 * h + 1))],
        out_specs=pl.BlockSpec((tq, MLA_V), lambda h, i, j: (i, h)),
        out_shape=jax.ShapeDtypeStruct((n_q, H_A * MLA_V), BF16),
        scratch_shapes=[pltpu.VMEM((tq, LANES), F32), pltpu.VMEM((tq, 2 * MLA_V), F32)],
        compiler_params=_cparams("parallel", "parallel", "arbitrary"),
        name=name,
    )(q, kv, kr, kv)


def _swa_kernel(*refs, band, nb):
    if band:
        (q_ref, c_ref, s1_ref, s2_ref, sink_ref, kc_ref, vc_ref, kp_ref, k0_ref, kn_ref, vp_ref, v0_ref, vn_ref,
         o_ref) = refs
    else:
        q_ref, c_ref, s1_ref, s2_ref, sink_ref, kc_ref, vc_ref, o_ref = refs
    i = pl.program_id(0)
    qb = q_ref.shape[0]
    cos, s1, s2 = c_ref[...], s1_ref[...], s2_ref[...]

    def roped(x):
        return (x * cos + pltpu.roll(x, LANES - HD_C // 4, axis=1) * s1 + pltpu.roll(x, HD_C // 4, axis=1) * s2)

    lane = lax.broadcasted_iota(jnp.int32, (1, LANES), 1)
    low = lane < HD_C
    if band:
        r = lax.broadcasted_iota(jnp.int32, (qb, qb), 0)
        c = lax.broadcasted_iota(jnp.int32, (qb, qb), 1)
        ok_prev = c - r + jnp.where(i > 0, 0, -4 * qb) >= qb - WINDOW
        ok_next = r - c + jnp.where(i < nb - 1, 0, -4 * qb) >= qb - WINDOW

    def masked(s, ok):
        return jnp.where(ok[None], s.reshape(G_C, qb, qb), -jnp.inf).reshape(G_C * qb, qb)

    for g in range(KV_C):
        grp, upper = g // 2, g % 2 == 1
        mine = ~low if upper else low

        def both_halves(x_ref):
            x = x_ref[:, grp * LANES:(grp + 1) * LANES]
            return jnp.where(mine, x, pltpu.roll(x, HD_C, axis=1))

        qs = []
        for cgrp in range(G_C // 2):
            col = (g * (G_C // 2) + cgrp) * LANES
            x = roped(q_ref[:, col:col + LANES])
            qs += [jnp.where(low, x, 0.0), jnp.where(low, 0.0, x)]
        qs = jnp.concatenate(qs, axis=0).astype(BF16)

        def scores(k_ref):
            return lax.dot_general(qs, both_halves(k_ref).astype(BF16), (((1,), (1,)), ((), ())),
                                   preferred_element_type=F32)

        parts = [(scores(kc_ref), vc_ref)]
        if band:
            parts += [(masked(scores(kp_ref), ok_prev), vp_ref), (scores(k0_ref), v0_ref),
                      (masked(scores(kn_ref), ok_next), vn_ref)]
        sink = sink_ref[g]
        m = sink
        for s, _ in parts:
            m = jnp.maximum(m, jnp.max(s, axis=-1, keepdims=True))
        l = jnp.exp2(sink - m)
        acc_lo = jnp.zeros((G_C * qb, LANES), F32)
        acc_hi = jnp.zeros((G_C * qb, LANES), F32)
        for s, v_ref in parts:
            p = jnp.exp2(s - m)
            l = l + jnp.sum(p, axis=-1, keepdims=True)
            p = p.astype(BF16)
            v2 = both_halves(v_ref)
            acc_lo = acc_lo + jnp.dot(p, jnp.where(low, v2, 0).astype(BF16), preferred_element_type=F32)
            acc_hi = acc_hi + jnp.dot(p, jnp.where(low, 0, v2).astype(BF16), preferred_element_type=F32)
        inv = 1.0 / l
        for cgrp in range(G_C // 2):
            lo_rows = slice((2 * cgrp) * qb, (2 * cgrp + 1) * qb)
            hi_rows = slice((2 * cgrp + 1) * qb, (2 * cgrp + 2) * qb)
            tile = acc_lo[lo_rows] * inv[lo_rows] + acc_hi[hi_rows] * inv[hi_rows]
            col = (g * (G_C // 2) + cgrp) * LANES
            o_ref[:, col:col + LANES] = tile.astype(o_ref.dtype)


def _swa_attention(p, q_off, tabs, kr, vv, sink_rows, *, n_ctx, band, name):
    T = p.shape[0]
    qb = 128
    assert WINDOW == qb and n_ctx % qb == 0 and q_off % W_C == 0
    cb = n_ctx // qb
    kvw = KV_C * HD_C
    nb, row0 = ((T - n_ctx) // qb, cb) if band else (cb, 0)
    tab = pl.BlockSpec((qb, LANES), lambda i: (i + row0, 0))
    in_specs = [pl.BlockSpec((qb, W_C), lambda i: (i + row0, q_off // W_C)), tab, tab, tab,
                pl.BlockSpec((KV_C, G_C * qb, 1), lambda i: (0, 0, 0)),
                pl.BlockSpec((n_ctx, kvw), lambda i: (0, 0)),
                pl.BlockSpec((n_ctx, kvw), lambda i: (0, 0))]
    args = [p, *tabs, sink_rows, kr, vv]
    if band:
        blk = lambda off: pl.BlockSpec((qb, kvw), lambda i: (cb + jnp.clip(i + off, 0, nb - 1), 0))
        in_specs += [blk(-1), blk(0), blk(1)] * 2
        args += [kr, kr, kr, vv, vv, vv]
    return pl.pallas_call(
        functools.partial(_swa_kernel, band=band, nb=nb),
        grid=(nb,),
        in_specs=in_specs,
        out_specs=pl.BlockSpec((qb, W_C), lambda i: (i, 0)),
        out_shape=jax.ShapeDtypeStruct((nb * qb, W_C), BF16),
        compiler_params=_cparams("parallel"),
        name=name,
    )(*args)


N_PAIR = H_B // 2


def _scan_kernel(rf_ref, vf_ref, kkf_ref, wf_ref, kdf_ref, bf_ref, rb_ref, vb_ref, kkb_ref, wb_ref, kdb_ref,
                 bb_ref, of_ref, ob_ref, sf_ref, sb_ref, *, tb):
    @pl.when(pl.program_id(0) == 0)
    def _():
        sf_ref[...] = jnp.zeros_like(sf_ref)
        sb_ref[...] = jnp.zeros_like(sb_ref)

    lane_r = lax.broadcasted_iota(jnp.int32, (LANES, LANES), 0)
    lane_c = lax.broadcasted_iota(jnp.int32, (LANES, LANES), 1)
    seg = (lane_r // N_B == lane_c // N_B).astype(BF16)
    on_diag = (lax.broadcasted_iota(jnp.int32, (N_B, LANES), 0)
               == lax.broadcasted_iota(jnp.int32, (N_B, LANES), 1) % N_B)
    diag = on_diag.astype(F32)
    diag16 = on_diag.astype(BF16)

    def segsum(x):
        return jnp.dot(x, seg, preferred_element_type=F32)

    def tile_of(x, p):
        return x[p * N_B:(p + 1) * N_B, :]

    def one_step(i, tiles, s_ref):
        r_t, v_t, kk_t, w_t, kd_t, b_t = ([x[i:i + 1, :] for x in op] for op in tiles)
        sa = segsum(jnp.concatenate([(s_ref[p] * kk_t[p]).astype(BF16) for p in range(N_PAIR)], axis=0))
        vcol = segsum(jnp.concatenate(
            [diag16 * jnp.broadcast_to(v_t[p], (N_B, LANES)).astype(BF16) for p in range(N_PAIR)], axis=0))
        new = []
        for p in range(N_PAIR):
            s = s_ref[p] * w_t[p] - tile_of(sa, p) * b_t[p] + tile_of(vcol, p) * kd_t[p]
            s_ref[p] = s
            new.append((s * r_t[p]).astype(BF16))
        oc = segsum(jnp.concatenate(new, axis=0))
        return [jnp.sum(diag * tile_of(oc, p), axis=0, keepdims=True) for p in range(N_PAIR)]

    def load8(refs, row0):
        return [[ref[pl.ds(row0, 8), p * LANES:(p + 1) * LANES] for p in range(N_PAIR)] for ref in refs]

    def store8(o_ref, row0, rows):
        for p in range(N_PAIR):
            o_ref[pl.ds(row0, 8), p * LANES:(p + 1) * LANES] = jnp.concatenate([rows[i][p] for i in range(8)], 0)

    def group(g, carry):
        f0 = pl.multiple_of(g * 8, 8)
        b0 = pl.multiple_of(tb - 8 - g * 8, 8)
        f_tiles = load8((rf_ref, vf_ref, kkf_ref, wf_ref, kdf_ref, bf_ref), f0)
        b_tiles = load8((rb_ref, vb_ref, kkb_ref, wb_ref, kdb_ref, bb_ref), b0)
        f_rows, b_rows = [None] * 8, [None] * 8
        for i in range(8):
            f_rows[i] = one_step(i, f_tiles, sf_ref)
            b_rows[7 - i] = one_step(7 - i, b_tiles, sb_ref)
        store8(of_ref, f0, f_rows)
        store8(ob_ref, b0, b_rows)
        return carry

    lax.fori_loop(0, tb // 8, group, 0)


def _rwkv_scan(r, v, kk, ws, kds, bs, *, n_ctx):
    T = r.shape[0]
    tb = 128
    nb, cb = T // tb, n_ctx // tb
    assert T % tb == 0 and n_ctx % tb == 0

    def back(j):
        return jnp.where(j < cb, cb - 1 - j, nb - 1 - (j - cb))

    fwd = pl.BlockSpec((tb, W_B), lambda j: (j, 0))
    bwd = pl.BlockSpec((tb, W_B), lambda j: (back(j), 0))
    return pl.pallas_call(
        functools.partial(_scan_kernel, tb=tb),
        grid=(nb,),
        in_specs=[fwd] * 6 + [bwd] * 6,
        out_specs=[fwd, bwd],
        out_shape=[jax.ShapeDtypeStruct((T, W_B), F32)] * 2,
        scratch_shapes=[pltpu.VMEM((N_PAIR, N_B, LANES), F32)] * 2,
        compiler_params=_cparams("arbitrary"),
        name="rwkv_scan",
    )(r, v, kk, ws[0], kds[0], bs[0], r, v, kk, ws[1], kds[1], bs[1])


def _merge_kernel(o_ref, w_ref, g_ref, out_ref, acc_ref):
    j = pl.program_id(2)
    part = jax.nn.sigmoid(g_ref[...]) * jnp.dot(o_ref[0], w_ref[0], preferred_element_type=F32)

    @pl.when(j == 0)
    def _():
        acc_ref[...] = part

    @pl.when(j > 0)
    def _():
        acc_ref[...] += part

    @pl.when(j == N_BRANCH - 1)
    def _():
        out_ref[...] = acc_ref[...].astype(out_ref.dtype)


def _merge(o3, w_branch, p, gate_off):
    _, T, W = o3.shape
    D = w_branch.shape[2]
    tm = _tile(T, (640, 512, 256, 128))
    tn = _tile(D, (1024, 512, 256, 128))
    assert gate_off % tn == 0
    gb, nj = gate_off // tn, D // tn
    return pl.pallas_call(
        _merge_kernel,
        grid=(T // tm, D // tn, N_BRANCH),
        in_specs=[pl.BlockSpec((1, tm, W), lambda i, n, j: (j, i, 0)),
                  pl.BlockSpec((1, W, tn), lambda i, n, j: (j, 0, n)),
                  pl.BlockSpec((tm, tn), lambda i, n, j: (i, gb + j * nj + n))],
        out_specs=pl.BlockSpec((tm, tn), lambda i, n, j: (i, n)),
        out_shape=jax.ShapeDtypeStruct((T, D), BF16),
        scratch_shapes=[pltpu.VMEM((tm, tn), F32)],
        compiler_params=_cparams("parallel", "parallel", "arbitrary"),
        name="merge",
    )(o3, w_branch, p)


def _router_kernel(x_ref, w_ref, b_ref, e_ref, g_ref):
    logits = jnp.dot(x_ref[...], w_ref[...], preferred_element_type=F32,
                     precision=lax.Precision.HIGHEST) + b_ref[...]
    lane = lax.broadcasted_iota(jnp.int32, logits.shape, 1)
    vals, idxs = [], []
    x = logits
    for _ in range(TOP_K):
        m = jnp.max(x, axis=-1, keepdims=True)
        idx = jnp.min(jnp.where(x == m, lane, LANES), axis=-1, keepdims=True)
        vals.append(m)
        idxs.append(idx)
        x = jnp.where(lane == idx, -jnp.inf, x)
    ex = [jnp.exp(v - vals[0]) for v in vals]
    tot = ex[0] + ex[1] + ex[2] + ex[3]
    e_out = jnp.zeros(logits.shape, jnp.int32)
    g_out = jnp.zeros(logits.shape, F32)
    for k in range(TOP_K):
        e_out = jnp.where(lane == k, idxs[k], e_out)
        g_out = jnp.where(lane == k, ex[k] / tot, g_out)
    e_ref[...] = e_out
    g_ref[...] = g_out


def _router(h, router_w, router_b):
    T, D = h.shape
    tm = _tile(T, (256, 128))
    w = jnp.zeros((D, LANES), F32).at[:, :N_EXPERTS].set(router_w)
    b = jnp.full((1, LANES), -1e30, F32).at[0, :N_EXPERTS].set(router_b)
    return pl.pallas_call(
        _router_kernel,
        grid=(T // tm,),
        in_specs=[pl.BlockSpec((tm, D), lambda i: (i, 0)),
                  pl.BlockSpec((D, LANES), lambda i: (0, 0)),
                  pl.BlockSpec((1, LANES), lambda i: (0, 0))],
        out_specs=[pl.BlockSpec((tm, LANES), lambda i: (i, 0))] * 2,
        out_shape=[jax.ShapeDtypeStruct((T, LANES), jnp.int32), jax.ShapeDtypeStruct((T, LANES), F32)],
        compiler_params=_cparams("parallel"),
        name="router_topk",
    )(h, w, b)


def _row_gather(src_ref, dst_ref, sem, index_of, n):
    def copy(r):
        return pltpu.make_async_copy(src_ref.at[pl.ds(index_of(r), 1), :], dst_ref.at[pl.ds(r, 1), :], sem)

    def start():
        lax.fori_loop(0, n, lambda r, c: (copy(r).start(), c)[1], 0)

    def wait():
        lax.fori_loop(0, n, lambda r, c: (copy(r).wait(), c)[1], 0)

    return start, wait


def _expert_kernel(be_ref, nu_ref, idx_ref, nidx_ref, h_ref, wgu_ref, bgu_ref, wd_ref, bd_ref, o_ref, xbuf, sem):
    i = pl.program_id(0)
    n_used = nu_ref[0]
    slot = lax.rem(i, 2)

    def gather(ids_ref, s):
        return _row_gather(h_ref, xbuf.at[s], sem.at[s], lambda r: ids_ref[0, 0, r], MOE_BLOCK)

    @pl.when(i == 0)
    def _():
        gather(idx_ref, 0)[0]()

    @pl.when(i + 1 < n_used)
    def _():
        gather(nidx_ref, 1 - slot)[0]()

    @pl.when(i < n_used)
    def _():
        gather(idx_ref, slot)[1]()
        gu = jnp.dot(xbuf[slot].astype(BF16), wgu_ref[0], preferred_element_type=F32) + bgu_ref[0]
        gate = jnp.minimum(gu[:, :D_EXPERT], SWIGLU_LIMIT)
        up = jnp.clip(gu[:, D_EXPERT:], -SWIGLU_LIMIT, SWIGLU_LIMIT)
        act = (up + 1.0) * gate * jax.nn.sigmoid(gate * SWIGLU_ALPHA)
        o_ref[...] = jnp.dot(act.astype(BF16), wd_ref[0], preferred_element_type=F32) + bd_ref[0]

    @pl.when(i >= n_used)
    def _():
        o_ref[...] = jnp.zeros_like(o_ref)


def _experts(h, row_tok, block_e, n_used, w_gu, b_gu, w_down, b_down):
    D = h.shape[1]
    n_rows = row_tok.shape[0]
    nb = n_rows // MOE_BLOCK
    ids = row_tok.reshape(nb, 1, MOE_BLOCK)
    grid_spec = pltpu.PrefetchScalarGridSpec(
        num_scalar_prefetch=2,
        grid=(nb,),
        in_specs=[pl.BlockSpec((1, 1, MOE_BLOCK), lambda i, be, nu: (i, 0, 0), memory_space=pltpu.SMEM),
                  pl.BlockSpec((1, 1, MOE_BLOCK), lambda i, be, nu: (jnp.minimum(i + 1, nb - 1), 0, 0),
                               memory_space=pltpu.SMEM),
                  pl.BlockSpec(memory_space=pl.ANY),
                  pl.BlockSpec((1, D, 2 * D_EXPERT), lambda i, be, nu: (be[i], 0, 0)),
                  pl.BlockSpec((1, 1, 2 * D_EXPERT), lambda i, be, nu: (be[i], 0, 0)),
                  pl.BlockSpec((1, D_EXPERT, D), lambda i, be, nu: (be[i], 0, 0)),
                  pl.BlockSpec((1, 1, D), lambda i, be, nu: (be[i], 0, 0))],
        out_specs=pl.BlockSpec((MOE_BLOCK, D), lambda i, be, nu: (i, 0)),
        scratch_shapes=[pltpu.VMEM((2, MOE_BLOCK, D), F32), pltpu.SemaphoreType.DMA((2,))],
    )
    return pl.pallas_call(
        _expert_kernel,
        grid_spec=grid_spec,
        out_shape=jax.ShapeDtypeStruct((n_rows, D), F32),
        compiler_params=_cparams("arbitrary"),
        name="experts",
    )(block_e, n_used, ids, ids, h, w_gu, b_gu.reshape(N_EXPERTS, 1, -1), w_down,
      b_down.reshape(N_EXPERTS, 1, -1))


def _combine_kernel(idx_ref, nidx_ref, g_ref, res_ref, mod_ref, y_ref, o_ref, buf, sem, *, ctx_rows, tm, nt):
    i = pl.program_id(0)
    slot = lax.rem(i, 2)

    def gathers(ids_ref, s):
        return [_row_gather(y_ref, buf.at[s, k], sem.at[s], functools.partial(lambda r, k: ids_ref[0, k, r], k=k), tm)
                for k in range(TOP_K)]

    @pl.when(i == 0)
    def _():
        for start, _ in gathers(idx_ref, 0):
            start()

    @pl.when(i + 1 < nt)
    def _():
        for start, _ in gathers(nidx_ref, 1 - slot):
            start()

    for _, wait in gathers(idx_ref, slot):
        wait()
    g = g_ref[...]
    y = g[:, 0:1] * buf[slot, 0]
    for k in range(1, TOP_K):
        y = y + g[:, k:k + 1] * buf[slot, k]
    rows = i * tm + lax.broadcasted_iota(jnp.int32, (tm, 1), 0)
    m = jnp.where(rows < ctx_rows, mod_ref[1:2, :], mod_ref[0:1, :])
    o_ref[...] = res_ref[...] + m * y


def _combine(yb, dest, gates, resid, mod, ctx_rows):
    T, D = resid.shape
    tm = _tile(T, (128,))
    nt = T // tm
    ids = jnp.swapaxes(dest.reshape(nt, tm, TOP_K), 1, 2)
    return pl.pallas_call(
        functools.partial(_combine_kernel, ctx_rows=ctx_rows, tm=tm, nt=nt),
        grid=(nt,),
        in_specs=[pl.BlockSpec((1, TOP_K, tm), lambda i: (i, 0, 0), memory_space=pltpu.SMEM),
                  pl.BlockSpec((1, TOP_K, tm), lambda i: (jnp.minimum(i + 1, nt - 1), 0, 0),
                               memory_space=pltpu.SMEM),
                  pl.BlockSpec((tm, LANES), lambda i: (i, 0)),
                  pl.BlockSpec((tm, D), lambda i: (i, 0)),
                  pl.BlockSpec((8, D), lambda i: (0, 0)),
                  pl.BlockSpec(memory_space=pl.ANY)],
        out_specs=pl.BlockSpec((tm, D), lambda i: (i, 0)),
        out_shape=jax.ShapeDtypeStruct((T, D), F32),
        scratch_shapes=[pltpu.VMEM((2, TOP_K, tm, D), F32), pltpu.SemaphoreType.DMA((2,))],
        compiler_params=_cparams("arbitrary"),
        name="moe_combine",
    )(ids, ids, gates, resid, mod, yb)


def _moe(h2, resid, mod, ctx_rows, router_w, router_b, w_gu, b_gu, w_down, b_down):
    T, D = h2.shape
    e128, g128 = _router(h2, router_w, router_b)
    top_e = e128[:, :TOP_K]
    onehot = (top_e[:, :, None] == jnp.arange(N_EXPERTS, dtype=jnp.int32)).astype(jnp.int32).sum(1)
    counts = onehot.sum(0)
    rank = jnp.cumsum(onehot, axis=0) - onehot
    pcounts = (counts + MOE_BLOCK - 1) // MOE_BLOCK * MOE_BLOCK
    pends = jnp.cumsum(pcounts)
    pstarts = pends - pcounts
    dest = (pstarts[top_e] + jnp.take_along_axis(rank, top_e, axis=1)).astype(jnp.int32)
    n_blocks = -(-T * TOP_K // MOE_BLOCK) + N_EXPERTS
    n_rows = n_blocks * MOE_BLOCK
    flat_dest = dest.reshape(-1)
    tok = jnp.repeat(jnp.arange(T, dtype=jnp.int32), TOP_K)
    row_tok = jnp.zeros((n_rows,), jnp.int32).at[flat_dest].set(tok)
    block_e = jnp.minimum(
        jnp.searchsorted(pends, jnp.arange(n_blocks, dtype=pends.dtype) * MOE_BLOCK, side='right'),
        N_EXPERTS - 1).astype(jnp.int32)
    n_used = (pends[-1] // MOE_BLOCK).astype(jnp.int32).reshape(1)
    yb = _experts(h2, row_tok, block_e, n_used, w_gu, b_gu, w_down, b_down)
    return _combine(yb, dest, g128, resid, mod, ctx_rows)


def _rope_tables(n_ctx, S):
    half = MLA_ROPE // 2
    inv = ROPE_THETA ** (-jnp.arange(0, half, 2, dtype=F32) / half)
    rows = jnp.repeat(jnp.arange(S // GRID_W), GRID_W).astype(F32)[:, None] * inv
    cols = jnp.tile(jnp.arange(GRID_W), S // GRID_W).astype(F32)[:, None] * inv
    ang = jnp.concatenate([rows, rows, cols, cols], -1)
    cos = jnp.concatenate([jnp.ones((n_ctx, MLA_ROPE), F32), jnp.cos(ang)], 0)
    sin = jnp.concatenate([jnp.zeros((n_ctx, MLA_ROPE), F32), jnp.sin(ang)], 0)
    first = (jnp.arange(LANES) % (MLA_ROPE // 2)) < MLA_ROPE // 4
    cos2, sin2 = (jnp.concatenate([t, t], -1) * (SWA_SCALE * LOG2E) for t in (cos, sin))
    swa_tabs = (cos2, jnp.where(first, -sin2, 0.0), jnp.where(first, 0.0, sin2))
    return cos, sin, swa_tabs


def _rope(x, cos, sin):
    q = MLA_ROPE // 4

    def rot(p):
        return jnp.concatenate([-p[..., q:], p[..., :q]], -1)

    xr = jnp.concatenate([rot(x[..., :2 * q]), rot(x[..., 2 * q:])], -1)
    return x * cos[:, None, :] + xr * sin[:, None, :]


def _pad_k(a, k):
    return jnp.pad(a, ((0, 0), (0, k - a.shape[1])))


LORA_W = 1024
O_WD_B, O_AD_F, O_AD_B, O_GD = R_W, 2 * R_W, 2 * R_W + R_A, 2 * R_W + 2 * R_A


def _seg_matrix():
    i = lax.broadcasted_iota(jnp.int32, (LANES, LANES), 0) // N_B
    j = lax.broadcasted_iota(jnp.int32, (LANES, LANES), 1) // N_B
    return (i == j).astype(BF16)


def _head_sums(x, seg):
    outs = []
    for c in range(x.shape[1] // LANES):
        rem = x[:, c * LANES:(c + 1) * LANES]
        tot = None
        for _ in range(3):
            hi = rem.astype(BF16)
            y = jnp.dot(hi, seg, preferred_element_type=F32)
            tot = y if tot is None else tot + y
            rem = rem - hi.astype(F32)
        outs.append(tot)
    return jnp.concatenate(outs, axis=1)


def _rwkv_prep_kernel(pr_ref, pk_ref, pv_ref, pl_ref, hp_r, hp_k, hp_v, hp_l, hn_r, hn_k, hn_v, hn_l,
                      mu_r, mu_k, mu_v, mu_l, kk_g, ka_g, w0_ref, a0_ref, wwf, wwb, waf, wab, wg,
                      r_o, v_o, kk_o, wf_o, kdf_o, bf_o, wb_o, kdb_o, bb_o, g_o, *, tm, n_ctx, n_tok):
    row = pl.program_id(0) * tm + lax.broadcasted_iota(jnp.int32, (tm, 1), 0)
    local = lax.broadcasted_iota(jnp.int32, (tm, 1), 0)
    has_prev = (row != 0) & (row != n_ctx)
    has_next = (row != n_ctx - 1) & (row != n_tok - 1)

    def shifted(x_ref, hp_ref, hn_ref, mu_ref):
        x = x_ref[...]
        prev = jnp.where(local == 0, hp_ref[7:8, :], pltpu.roll(x, 1, axis=0))
        nxt = jnp.where(local == tm - 1, hn_ref[0:1, :], pltpu.roll(x, tm - 1, axis=0))
        prev = jnp.where(has_prev, prev, 0.0)
        nxt = jnp.where(has_next, nxt, 0.0)
        return x + mu_ref[0:1, :] * (prev - x) + mu_ref[1:2, :] * (nxt - x)

    r = shifted(pr_ref, hp_r, hn_r, mu_r)
    k = shifted(pk_ref, hp_k, hn_k, mu_k)
    v = shifted(pv_ref, hp_v, hn_v, mu_v)
    lx = shifted(pl_ref, hp_l, hn_l, mu_l)
    seg = _seg_matrix()
    kk = k * kk_g[...]
    kk = kk * lax.rsqrt(_head_sums(kk * kk, seg) + 1e-12)
    r_o[...] = r
    v_o[...] = v
    kk_o[...] = kk
    th = jnp.tanh(lx[:, 0:2 * LANES]).astype(BF16)
    lin = lx[:, LANES:3 * LANES].astype(BF16)
    w_pre = (jnp.dot(th[:, 0:LANES], wwf[...], preferred_element_type=F32),
             jnp.dot(th, wwb[...], preferred_element_type=F32))
    a_pre = (jnp.dot(lin, waf[...], preferred_element_type=F32),
             jnp.dot(lin[:, LANES:], wab[...], preferred_element_type=F32))
    for d, (w_o, kd_o, b_o) in enumerate(((wf_o, kdf_o, bf_o), (wb_o, kdb_o, bb_o))):
        wp = w0_ref[d:d + 1, :] + w_pre[d]
        w_o[...] = jnp.exp(-jnp.exp(-jax.nn.softplus(-wp) - 0.5))
        a = jax.nn.sigmoid(a0_ref[d:d + 1, :] + a_pre[d])
        kd_o[...] = k * (1.0 + (a - 1.0) * ka_g[...])
        b_o[...] = kk * a
    gd = jax.nn.sigmoid(lx[:, O_GD:O_GD + R_G]).astype(BF16)
    g_o[...] = jnp.dot(gd, wg[...], preferred_element_type=F32)


def _rows8(a):
    return jnp.pad(a, ((0, 8 - a.shape[0]), (0, 0))).astype(F32)


def _rwkv_prep(p, lo, lp, n_ctx):
    T = p.shape[0]
    tm = 128
    assert T % tm == 0 and n_ctx % tm == 0
    hb = tm // 8
    last8 = T // 8 - 1
    main = lambda w, off: pl.BlockSpec((tm, w), lambda i: (i, off // w))
    halo_p = lambda w, off: pl.BlockSpec((8, w), lambda i: (jnp.maximum(i * hb - 1, 0), off // w))
    halo_n = lambda w, off: pl.BlockSpec((8, w), lambda i: (jnp.minimum((i + 1) * hb, last8), off // w))
    groups = ((W_B, lo['r']), (W_B, lo['k']), (W_B, lo['v']), (LORA_W, lo['lora']))
    for w, off in groups:
        assert off % w == 0
    full = lambda a: pl.BlockSpec(a.shape, lambda i: (0, 0))
    mu = lp['rwkv_mu']
    mus = [_rows8(mu[:, 0:W_B]), _rows8(mu[:, W_B:2 * W_B]), _rows8(mu[:, 2 * W_B:3 * W_B]),
           _rows8(jnp.pad(mu[:, 3 * W_B:], ((0, 0), (0, LORA_W - (C_RWKV - 3 * W_B)))))]

    def slot(w, lo_row, n_rows):
        return jnp.pad(w, ((lo_row, n_rows - lo_row - w.shape[0]), (0, 0))).astype(BF16)

    params = mus + [lp['rwkv_k_k'].reshape(1, W_B), lp['rwkv_k_a'].reshape(1, W_B),
                    _rows8(lp['rwkv_w0']), _rows8(lp['rwkv_a0']),
                    slot(lp['rwkv_w_up'][0], 0, LANES), slot(lp['rwkv_w_up'][1], O_WD_B, 2 * LANES),
                    slot(lp['rwkv_a_up'][0], O_AD_F - LANES, 2 * LANES),
                    slot(lp['rwkv_a_up'][1], O_AD_B - 2 * LANES, LANES),
                    lp['rwkv_g_up'].astype(BF16)]
    in_specs = ([main(w, off) for w, off in groups] + [halo_p(w, off) for w, off in groups]
                + [halo_n(w, off) for w, off in groups] + [full(a) for a in params])
    outs = pl.pallas_call(
        functools.partial(_rwkv_prep_kernel, tm=tm, n_ctx=n_ctx, n_tok=T),
        grid=(T // tm,),
        in_specs=in_specs,
        out_specs=[pl.BlockSpec((tm, W_B), lambda i: (i, 0))] * 10,
        out_shape=[jax.ShapeDtypeStruct((T, W_B), F32)] * 10,
        compiler_params=_cparams("parallel"),
        name="rwkv_prep",
    )(*([p] * 12 + params))
    return outs


def _rwkv_out_kernel(of_ref, ob_ref, r_ref, v_ref, kdf_ref, kdb_ref, g_ref, lng_ref, lnb_ref, rk_ref, o_ref):
    seg = _seg_matrix()
    o = of_ref[...] + ob_ref[...]
    d = o - _head_sums(o, seg) * (1.0 / N_B)
    var = _head_sums(d * d, seg) * (1.0 / N_B)
    on = d * lax.rsqrt(var + GN_EPS) * lng_ref[...] + lnb_ref[...]
    coef = _head_sums(r_ref[...] * (kdf_ref[...] + kdb_ref[...]) * rk_ref[...], seg)
    o_ref[...] = ((on + coef * v_ref[...]) * g_ref[...]).astype(o_ref.dtype)


def _rwkv_out(o_f, o_b, r, v, kd_f, kd_b, g, lp):
    T = r.shape[0]
    tm = _tile(T, (128,))
    blk = pl.BlockSpec((tm, W_B), lambda i: (i, 0))
    par = pl.BlockSpec((1, W_B), lambda i: (0, 0))
    return pl.pallas_call(
        _rwkv_out_kernel,
        grid=(T // tm,),
        in_specs=[blk] * 7 + [par] * 3,
        out_specs=blk,
        out_shape=jax.ShapeDtypeStruct((T, W_B), BF16),
        compiler_params=_cparams("parallel"),
        name="rwkv_out",
    )(o_f, o_b, r, v, kd_f, kd_b, g, lp['rwkv_ln_g'].reshape(1, W_B), lp['rwkv_ln_b'].reshape(1, W_B),
      lp['rwkv_r_k'].reshape(1, W_B))


def _rwkv_mixer(p, lo, lp, n_ctx):
    r, v, kk, w_f, kd_f, b_f, w_b, kd_b, b_b, g = _rwkv_prep(p, lo, lp, n_ctx)
    o_f, o_b = _rwkv_scan(r, v, kk, (w_f, w_b), (kd_f, kd_b), (b_f, b_b), n_ctx=n_ctx)
    return _rwkv_out(o_f, o_b, r, v, kd_f, kd_b, g, lp)


def _in_layout(D):
    o = {}
    o['g'] = 0
    o['r'] = N_BRANCH * D
    o['k'] = o['r'] + W_B
    o['v'] = o['k'] + W_B
    o['sq'] = o['v'] + W_B
    o['q'] = o['sq'] + W_C
    o['lora'] = o['q'] + Q_LORA
    o['kv'] = o['lora'] + LORA_W
    o['sk'] = o['kv'] + KV_LORA
    o['sv'] = o['sk'] + KV_C * HD_C
    o['kr'] = o['sv'] + KV_C * HD_C
    o['end'] = -(-(o['kr'] + MLA_ROPE) // 1024) * 1024
    return o


def _prep_w_in(w_in, D):
    s = [int(v) for v in np.cumsum([0, Q_LORA, KV_LORA, MLA_ROPE, C_RWKV, W_C, KV_C * HD_C, KV_C * HD_C,
                                    N_BRANCH * D])]
    seg = lambda i: w_in[:, s[i]:s[i + 1]]
    rw = seg(3)
    zeros = lambda n: jnp.zeros((w_in.shape[0], n), w_in.dtype)
    lo = _in_layout(D)
    cols = [seg(7), rw[:, :3 * W_B], seg(4), seg(0), rw[:, 3 * W_B:], zeros(LORA_W - (C_RWKV - 3 * W_B)), seg(1),
            seg(5), seg(6), seg(2), zeros(lo['end'] - lo['kr'] - MLA_ROPE)]
    return jnp.concatenate(cols, -1).astype(BF16)


def _rot_cols(w):
    q = MLA_ROPE // 4
    return jnp.concatenate([-w[..., q:2 * q], w[..., :q], -w[..., 3 * q:], w[..., 2 * q:3 * q]], -1)


def _prep_wq_up(wq_up):
    w = wq_up.reshape(Q_LORA, H_A, MLA_NOPE + MLA_ROPE)
    w = jnp.concatenate([w, _rot_cols(w[..., MLA_NOPE:])], -1)
    return w.reshape(Q_LORA, H_A * MLA_QW).astype(BF16)


def _layer(xall, mod, lp, rope, n_ctx):
    T, D = xall.shape
    S = T - n_ctx
    cos, sin, swa_tabs = rope
    md = [mod[:, j * D:(j + 1) * D] for j in range(6)]
    lo = _in_layout(D)

    h1 = _rownorm(xall, lp['norm1_g'], shift=md[0], scale=md[1], ctx_rows=n_ctx, name="norm1")
    w_in = _prep_w_in(lp['w_in'], D)
    p = _matmul(h1, w_in, tm=_tile(T, (1280, 640, 256, 128)), tn=512, name="w_in")

    qn = _rownorm(p, lp['mla_qn_g'], col_off=lo['q'], width=Q_LORA, name="mla_qnorm")
    q_tab = jnp.concatenate([jnp.ones((T, MLA_NOPE), F32), cos, sin], -1) * (MLA_SCALE * LOG2E)
    q = _matmul(qn, _prep_wq_up(lp['mla_wq_up']), row_table=q_tab, out_dtype=BF16, name="mla_q_up")
    kvn = _rownorm(p, lp['mla_kvn_g'], col_off=lo['kv'], width=KV_LORA, name="mla_kvnorm")
    kv = _matmul(kvn, lp['mla_wkv_up'].astype(BF16), out_dtype=BF16, name="mla_kv_up")
    kr = _rope(p[:, None, lo['kr']:lo['kr'] + MLA_ROPE], cos, sin)[:, 0]
    kr = jnp.concatenate([kr, kr], -1).astype(BF16)
    oa_x = _mla_attention(q[n_ctx:], kv, kr, n_q=S, n_keys=T, name="mla_latent")
    oa_c = _mla_attention(q[:n_ctx], kv, kr, n_q=n_ctx, n_keys=n_ctx, name="mla_context")
    o_a = jnp.concatenate([oa_c, oa_x], 0)

    o_b = _rwkv_mixer(p, lo, lp, n_ctx)

    kvw = KV_C * HD_C
    kc = _rope(p[:, lo['sk']:lo['sk'] + kvw].reshape(T, KV_C, HD_C), cos, sin).reshape(T, kvw)
    vc = p[:, lo['sv']:lo['sv'] + kvw]
    sink_rows = jnp.repeat(lp['swa_sink'].astype(F32).reshape(KV_C, G_C) * LOG2E, 128, axis=1)
    sink_rows = sink_rows.reshape(KV_C, G_C * 128, 1)
    oc_x = _swa_attention(p, lo['sq'], swa_tabs, kc, vc, sink_rows, n_ctx=n_ctx, band=True, name="swa_latent")
    oc_c = _swa_attention(p, lo['sq'], swa_tabs, kc, vc, sink_rows, n_ctx=n_ctx, band=False, name="swa_context")
    o_c = jnp.concatenate([oc_c, oc_x], 0)

    m = _merge(jnp.stack([o_a, o_b, o_c], 0), lp['w_branch'].astype(BF16), p, lo['g'])
    x1 = _matmul(m, lp['w_out'].astype(BF16), resid=xall, mod=md[2], ctx_rows=n_ctx, name="w_out")
    h2 = _rownorm(x1, lp['norm2_g'], shift=md[3], scale=md[4], ctx_rows=n_ctx, out_dtype=F32, name="norm2")
    return _moe(h2, x1, md[5], n_ctx, lp['router_w'], lp['router_b'], lp['exp_w_gu'].astype(BF16),
                lp['exp_b_gu'], lp['exp_w_down'].astype(BF16), lp['exp_b_down'])


def kernel(x, c, ctx, c_ctx, ada_w, ada_b, norm1_g, w_in, mla_qn_g, mla_kvn_g, mla_wq_up, mla_wkv_up, rwkv_mu, rwkv_w0, rwkv_w_up, rwkv_a0, rwkv_a_up, rwkv_g_up, rwkv_k_k, rwkv_k_a, rwkv_r_k, rwkv_ln_g, rwkv_ln_b, swa_sink, w_branch, w_out, norm2_g, router_w, router_b, exp_w_gu, exp_b_gu, exp_w_down, exp_b_down, final_g):
    B, S, D = x.shape
    n_ctx = ctx.shape[1]
    assert B == 1 and c.shape[0] == 1
    depth = ada_w.shape[0]
    rope = _rope_tables(n_ctx, S)
    xall = jnp.concatenate([ctx[0], x[0]], 0)
    cond = jax.nn.silu(jnp.concatenate([c, c_ctx[None], jnp.zeros((6, D), F32)], 0))
    for i in range(depth):
        lp = dict(norm1_g=norm1_g[i], w_in=w_in[i],
                  mla_qn_g=mla_qn_g[i], mla_kvn_g=mla_kvn_g[i], mla_wq_up=mla_wq_up[i], mla_wkv_up=mla_wkv_up[i],
                  rwkv_mu=rwkv_mu[i], rwkv_w0=rwkv_w0[i], rwkv_w_up=rwkv_w_up[i], rwkv_a0=rwkv_a0[i],
                  rwkv_a_up=rwkv_a_up[i], rwkv_g_up=rwkv_g_up[i], rwkv_k_k=rwkv_k_k[i], rwkv_k_a=rwkv_k_a[i],
                  rwkv_r_k=rwkv_r_k[i], rwkv_ln_g=rwkv_ln_g[i], rwkv_ln_b=rwkv_ln_b[i], swa_sink=swa_sink[i],
                  w_branch=w_branch[i], w_out=w_out[i], norm2_g=norm2_g[i], router_w=router_w[i],
                  router_b=router_b[i], exp_w_gu=exp_w_gu[i], exp_b_gu=exp_b_gu[i],
                  exp_w_down=exp_w_down[i], exp_b_down=exp_b_down[i])
        mod = _matmul(cond, ada_w, b_slab=i, bias=ada_b[i], tm=8, tn=1024, name="adaln")
        xall = _layer(xall, mod, lp, rope, n_ctx)
    out = _rownorm(xall, final_g, row_off=n_ctx, rows=S, out_dtype=F32, name="final_norm")
    return out.reshape(B, S, D)
```

```python
import functools

import numpy as np
import jax
import jax.numpy as jnp
from jax import lax
from jax.experimental import pallas as pl
from jax.experimental.pallas import tpu as pltpu

F32 = jnp.float32
BF16 = jnp.bfloat16

GRID_W = 64
ROPE_THETA = 10000.0
NORM_EPS = 1e-6
H_A, MLA_NOPE, MLA_ROPE, MLA_V = 16, 128, 64, 128
Q_LORA, KV_LORA = 1024, 512
MLA_SCALE = (MLA_NOPE + MLA_ROPE) ** -0.5
LOG2E = 1.4426950408889634
MLA_QW = 256
H_B, N_B = 32, 64
W_B = H_B * N_B
R_W, R_A, R_G = 96, 96, 256
GN_EPS = 64e-5
H_C, KV_C, HD_C = 32, 4, 64
G_C = H_C // KV_C
WINDOW = 128
W_C = H_C * HD_C
SWA_SCALE = HD_C ** -0.5
N_BRANCH = 3
N_EXPERTS, TOP_K, D_EXPERT = 32, 4, 512
SWIGLU_LIMIT, SWIGLU_ALPHA = 7.0, 1.702
MOE_BLOCK = 128
C_RWKV = 3 * W_B + 2 * R_W + 2 * R_A + R_G

LANES = 128
VMEM_LIMIT = 56 * 1024 * 1024


def _cparams(*sem):
    return pltpu.CompilerParams(dimension_semantics=sem, vmem_limit_bytes=VMEM_LIMIT)


def _lane_tile(x, k):
    return x if k == 1 else jnp.concatenate([x] * k, axis=1)


def _tile(n, prefs):
    for t in prefs:
        if n % t == 0:
            return t
    return n


def _mm_kernel(*refs, nk, has_bias, resid, has_tab, ctx_rows, tm, precision):
    it = iter(refs)
    a_ref, b_ref = next(it), next(it)
    bias_ref = next(it) if has_bias else None
    res_ref, mod_ref = (next(it), next(it)) if resid else (None, None)
    tab_ref = next(it) if has_tab else None
    o_ref = next(it)
    acc_ref = next(it) if nk > 1 else None

    if precision is None:
        part = jnp.dot(a_ref[...].astype(BF16), b_ref[...].astype(BF16), preferred_element_type=F32)
    else:
        part = jnp.dot(a_ref[...], b_ref[...], preferred_element_type=F32, precision=precision)

    def finish(acc):
        if has_bias:
            acc = acc + bias_ref[...]
        if resid:
            rows = pl.program_id(0) * tm + lax.broadcasted_iota(jnp.int32, (tm, 1), 0)
            m = jnp.where(rows < ctx_rows, mod_ref[1:2, :], mod_ref[0:1, :])
            acc = res_ref[...] + m * acc
        if has_tab:
            acc = acc * _lane_tile(tab_ref[...], acc.shape[1] // tab_ref.shape[1])
        o_ref[...] = acc.astype(o_ref.dtype)

    if nk == 1:
        finish(part)
    else:
        k = pl.program_id(2)

        @pl.when(k == 0)
        def _():
            acc_ref[...] = part

        @pl.when(k > 0)
        def _():
            acc_ref[...] += part

        @pl.when(k == nk - 1)
        def _():
            finish(acc_ref[...])


def _matmul(a, b, *, a_col_off=0, k_dim=None, bias=None, resid=None, mod=None, ctx_rows=0, row_table=None,
            b_slab=0, out_dtype=F32, tm=None, tn=None, tk=None, precision=None, name="matmul"):
    M = a.shape[0]
    K, N = b.shape[-2:]
    if k_dim is None:
        assert a.shape[1] == K and a_col_off == 0
    tm = tm or _tile(M, (640, 512, 256, 128))
    tn = tn or _tile(N, (1024, 896, 512, 256, 128))
    tk = tk or K
    assert M % tm == 0 and N % tn == 0 and K % tk == 0 and a_col_off % tk == 0
    nk = K // tk
    koff = a_col_off // tk
    if b.ndim == 3:
        b_spec = pl.BlockSpec((None, tk, tn), lambda i, j, k: (b_slab, k, j))
    else:
        b_spec = pl.BlockSpec((tk, tn), lambda i, j, k: (k, j))
    in_specs = [pl.BlockSpec((tm, tk), lambda i, j, k: (i, k + koff)), b_spec]
    args = [a, b]
    if bias is not None:
        in_specs.append(pl.BlockSpec((1, tn), lambda i, j, k: (0, j)))
        args.append(bias.reshape(1, N).astype(F32))
    if resid is not None:
        in_specs.append(pl.BlockSpec((tm, tn), lambda i, j, k: (i, j)))
        in_specs.append(pl.BlockSpec((8, tn), lambda i, j, k: (0, j)))
        args += [resid, mod]
    if row_table is not None:
        assert tn % row_table.shape[1] == 0
        in_specs.append(pl.BlockSpec((tm, row_table.shape[1]), lambda i, j, k: (i, 0)))
        args.append(row_table)
    kern = functools.partial(_mm_kernel, nk=nk, has_bias=bias is not None, resid=resid is not None,
                             has_tab=row_table is not None, ctx_rows=ctx_rows, tm=tm, precision=precision)
    return pl.pallas_call(
        kern,
        grid=(M // tm, N // tn, nk),
        in_specs=in_specs,
        out_specs=pl.BlockSpec((tm, tn), lambda i, j, k: (i, j)),
        out_shape=jax.ShapeDtypeStruct((M, N), out_dtype),
        scratch_shapes=[pltpu.VMEM((tm, tn), F32)] if nk > 1 else [],
        compiler_params=_cparams("parallel", "parallel", "arbitrary"),
        name=name,
    )(*args)


def _norm_kernel(*refs, eps, modulated, ctx_rows, tm):
    if modulated:
        x_ref, g_ref, sh_ref, sc_ref, o_ref = refs
    else:
        x_ref, g_ref, o_ref = refs
    x = x_ref[...].astype(F32)
    y = x * lax.rsqrt(jnp.mean(x * x, axis=-1, keepdims=True) + eps) * g_ref[...]
    if modulated:
        rows = pl.program_id(0) * tm + lax.broadcasted_iota(jnp.int32, (tm, 1), 0)
        isctx = rows < ctx_rows
        sc = jnp.where(isctx, sc_ref[1:2, :], sc_ref[0:1, :])
        sh = jnp.where(isctx, sh_ref[1:2, :], sh_ref[0:1, :])
        y = y * (1.0 + sc) + sh
    o_ref[...] = y.astype(o_ref.dtype)


def _rownorm(x, g, *, col_off=0, width=None, row_off=0, rows=None, shift=None, scale=None,
             ctx_rows=0, out_dtype=BF16, eps=NORM_EPS, name="rownorm"):
    width = width or x.shape[1]
    rows = rows or x.shape[0]
    tm = _tile(rows, (256, 128))
    assert col_off % width == 0 and row_off % tm == 0 and rows % tm == 0
    cb, rb = col_off // width, row_off // tm
    modulated = shift is not None
    in_specs = [pl.BlockSpec((tm, width), lambda i: (i + rb, cb)),
                pl.BlockSpec((1, width), lambda i: (0, 0))]
    args = [x, g.reshape(1, width).astype(F32)]
    if modulated:
        in_specs += [pl.BlockSpec((8, width), lambda i: (0, 0))] * 2
        args += [shift, scale]
    kern = functools.partial(_norm_kernel, eps=eps, modulated=modulated, ctx_rows=ctx_rows, tm=tm)
    return pl.pallas_call(
        kern,
        grid=(rows // tm,),
        in_specs=in_specs,
        out_specs=pl.BlockSpec((tm, width), lambda i: (i, 0)),
        out_shape=jax.ShapeDtypeStruct((rows, width), out_dtype),
        compiler_params=_cparams("parallel"),
        name=name,
    )(*args)


def _flash_kernel(q_ref, kn_ref, kr_ref, v_ref, o_ref, m_ref, acc_ref, *, nk, sub):
    kj = pl.program_id(2)

    @pl.when(kj == 0)
    def _():
        m_ref[...] = jnp.full_like(m_ref, -jnp.inf)
        acc_ref[...] = jnp.zeros_like(acc_ref)

    q = q_ref[...]
    m, acc = m_ref[...], acc_ref[...]
    ones = jnp.ones((sub, MLA_V), BF16)
    for c in range(kn_ref.shape[0] // sub):
        rows = slice(c * sub, (c + 1) * sub)
        k = jnp.concatenate([kn_ref[rows, :], kr_ref[rows, :]], axis=-1)
        s = lax.dot_general(q, k, (((1,), (1,)), ((), ())), preferred_element_type=F32)
        m_new = jnp.maximum(m, jnp.max(s, axis=-1, keepdims=True))
        p = jnp.exp2(s - _lane_tile(m_new, sub // LANES)).astype(BF16)
        v1 = jnp.concatenate([v_ref[rows, :], ones], axis=-1)
        acc = _lane_tile(jnp.exp2(m - m_new), 2) * acc + jnp.dot(p, v1, preferred_element_type=F32)
        m = m_new
    m_ref[...] = m
    acc_ref[...] = acc

    @pl.when(kj == nk - 1)
    def _():
        acc = acc_ref[...]
        o_ref[...] = (acc[:, :MLA_V] / acc[:, MLA_V:]).astype(o_ref.dtype)


def _mla_attention(q, kv, kr, *, n_q, n_keys, name):
    tq = _tile(n_q, (2048, 1024, 512, 256, 128))
    tk = _tile(n_keys, (1280, 640, 512, 256, 128))
    nk = n_keys // tk
    sub = _tile(tk, (256, 128))
    return pl.pallas_call(
        functools.partial(_flash_kernel, nk=nk, sub=sub),
        grid=(H_A, n_q // tq, nk),
        in_specs=[pl.BlockSpec((tq, MLA_QW), lambda h, i, j: (i, h)),
                  pl.BlockSpec((tk, MLA_NOPE), lambda h, i, j: (j, 2 * h)),
                  pl.BlockSpec((tk, LANES), lambda h, i, j: (j, 0)),
                  pl.BlockSpec((tk, MLA_V), lambda h, i, j: (j, 2 * h + 1))],
        out_specs=pl.BlockSpec((tq, MLA_V), lambda h, i, j: (i, h)),
        out_shape=jax.ShapeDtypeStruct((n_q, H_A * MLA_V), BF16),
        scratch_shapes=[pltpu.VMEM((tq, LANES), F32), pltpu.VMEM((tq, 2 * MLA_V), F32)],
        compiler_params=_cparams("parallel", "parallel", "arbitrary"),
        name=name,
    )(q, kv, kr, kv)


def _swa_kernel(*refs, band, nb):
    if band:
        (q_ref, c_ref, s1_ref, s2_ref, sink_ref, kc_ref, vc_ref, kp_ref, k0_ref, kn_ref, vp_ref, v0_ref, vn_ref,
         o_ref) = refs
    else:
        q_ref, c_ref, s1_ref, s2_ref, sink_ref, kc_ref, vc_ref, o_ref = refs
    i = pl.program_id(0)
    qb = q_ref.shape[0]
    cos, s1, s2 = c_ref[...], s1_ref[...], s2_ref[...]

    def roped(x):
        return (x * cos + pltpu.roll(x, LANES - HD_C // 4, axis=1) * s1 + pltpu.roll(x, HD_C // 4, axis=1) * s2)

    lane = lax.broadcasted_iota(jnp.int32, (1, LANES), 1)
    low = lane < HD_C
    if band:
        r = lax.broadcasted_iota(jnp.int32, (qb, qb), 0)
        c = lax.broadcasted_iota(jnp.int32, (qb, qb), 1)
        ok_prev = c - r + jnp.where(i > 0, 0, -4 * qb) >= qb - WINDOW
        ok_next = r - c + jnp.where(i < nb - 1, 0, -4 * qb) >= qb - WINDOW

    def masked(s, ok):
        return jnp.where(ok[None], s.reshape(G_C, qb, qb), -jnp.inf).reshape(G_C * qb, qb)

    for g in range(KV_C):
        grp, upper = g // 2, g % 2 == 1
        mine = ~low if upper else low

        def both_halves(x_ref):
            x = x_ref[:, grp * LANES:(grp + 1) * LANES]
            return jnp.where(mine, x, pltpu.roll(x, HD_C, axis=1))

        qs = []
        for cgrp in range(G_C // 2):
            col = (g * (G_C // 2) + cgrp) * LANES
            x = roped(q_ref[:, col:col + LANES])
            qs += [jnp.where(low, x, 0.0), jnp.where(low, 0.0, x)]
        qs = jnp.concatenate(qs, axis=0).astype(BF16)

        def scores(k_ref):
            return lax.dot_general(qs, both_halves(k_ref).astype(BF16), (((1,), (1,)), ((), ())),
                                   preferred_element_type=F32)

        parts = [(scores(kc_ref), vc_ref)]
        if band:
            parts += [(masked(scores(kp_ref), ok_prev), vp_ref), (scores(k0_ref), v0_ref),
                      (masked(scores(kn_ref), ok_next), vn_ref)]
        sink = sink_ref[g]
        m = sink
        for s, _ in parts:
            m = jnp.maximum(m, jnp.max(s, axis=-1, keepdims=True))
        l = jnp.exp2(sink - m)
        acc = [jnp.zeros((qb, LANES), F32)] * G_C
        for s, v_ref in parts:
            p = jnp.exp2(s - m)
            l = l + jnp.sum(p, axis=-1, keepdims=True)
            p = p.astype(BF16)
            v2 = both_halves(v_ref)
            v_half = (jnp.where(low, v2, 0).astype(BF16), jnp.where(low, 0, v2).astype(BF16))
            acc = [acc[j] + jnp.dot(p[j * qb:(j + 1) * qb], v_half[j % 2], preferred_element_type=F32)
                   for j in range(G_C)]
        inv = 1.0 / l
        for cgrp in range(G_C // 2):
            lo_rows = slice((2 * cgrp) * qb, (2 * cgrp + 1) * qb)
            hi_rows = slice((2 * cgrp + 1) * qb, (2 * cgrp + 2) * qb)
            tile = acc[2 * cgrp] * inv[lo_rows] + acc[2 * cgrp + 1] * inv[hi_rows]
            col = (g * (G_C // 2) + cgrp) * LANES
            o_ref[:, col:col + LANES] = tile.astype(o_ref.dtype)


def _swa_attention(p, q_off, tabs, kr, vv, sink_rows, *, n_ctx, band, name):
    T = p.shape[0]
    qb = 128
    assert WINDOW == qb and n_ctx % qb == 0 and q_off % W_C == 0
    cb = n_ctx // qb
    kvw = KV_C * HD_C
    nb, row0 = ((T - n_ctx) // qb, cb) if band else (cb, 0)
    tab = pl.BlockSpec((qb, LANES), lambda i: (i + row0, 0))
    in_specs = [pl.BlockSpec((qb, W_C), lambda i: (i + row0, q_off // W_C)), tab, tab, tab,
                pl.BlockSpec((KV_C, G_C * qb, 1), lambda i: (0, 0, 0)),
                pl.BlockSpec((n_ctx, kvw), lambda i: (0, 0)),
                pl.BlockSpec((n_ctx, kvw), lambda i: (0, 0))]
    args = [p, *tabs, sink_rows, kr, vv]
    if band:
        blk = lambda off: pl.BlockSpec((qb, kvw), lambda i: (cb + jnp.clip(i + off, 0, nb - 1), 0))
        in_specs += [blk(-1), blk(0), blk(1)] * 2
        args += [kr, kr, kr, vv, vv, vv]
    return pl.pallas_call(
        functools.partial(_swa_kernel, band=band, nb=nb),
        grid=(nb,),
        in_specs=in_specs,
        out_specs=pl.BlockSpec((qb, W_C), lambda i: (i, 0)),
        out_shape=jax.ShapeDtypeStruct((nb * qb, W_C), BF16),
        compiler_params=_cparams("parallel"),
        name=name,
    )(*args)


N_PAIR = H_B // 2


def _scan_kernel(rf_ref, vf_ref, kkf_ref, wf_ref, kdf_ref, bf_ref, rb_ref, vb_ref, kkb_ref, wb_ref, kdb_ref,
                 bb_ref, of_ref, ob_ref, sf_ref, sb_ref, *, tb):
    @pl.when(pl.program_id(0) == 0)
    def _():
        sf_ref[...] = jnp.zeros_like(sf_ref)
        sb_ref[...] = jnp.zeros_like(sb_ref)

    lane_r = lax.broadcasted_iota(jnp.int32, (LANES, LANES), 0)
    lane_c = lax.broadcasted_iota(jnp.int32, (LANES, LANES), 1)
    seg = (lane_r // N_B == lane_c // N_B).astype(BF16)
    on_diag = (lax.broadcasted_iota(jnp.int32, (N_B, LANES), 0)
               == lax.broadcasted_iota(jnp.int32, (N_B, LANES), 1) % N_B)
    diag = on_diag.astype(F32)
    diag16 = on_diag.astype(BF16)

    def segsum(x):
        return jnp.dot(x, seg, preferred_element_type=F32)

    def tile_of(x, p):
        return x[p * N_B:(p + 1) * N_B, :]

    def one_step(i, tiles, s_ref):
        r_t, v_t, kk_t, w_t, kd_t, b_t = ([x[i:i + 1, :] for x in op] for op in tiles)
        sa = segsum(jnp.concatenate([(s_ref[p] * kk_t[p]).astype(BF16) for p in range(N_PAIR)], axis=0))
        vcol = segsum(jnp.concatenate(
            [diag16 * jnp.broadcast_to(v_t[p], (N_B, LANES)).astype(BF16) for p in range(N_PAIR)], axis=0))
        new = []
        for p in range(N_PAIR):
            s = s_ref[p] * w_t[p] - tile_of(sa, p) * b_t[p] + tile_of(vcol, p) * kd_t[p]
            s_ref[p] = s
            new.append((s * r_t[p]).astype(BF16))
        oc = segsum(jnp.concatenate(new, axis=0))
        return [jnp.sum(diag * tile_of(oc, p), axis=0, keepdims=True) for p in range(N_PAIR)]

    def load8(refs, row0):
        return [[ref[pl.ds(row0, 8), p * LANES:(p + 1) * LANES] for p in range(N_PAIR)] for ref in refs]

    def store8(o_ref, row0, rows):
        for p in range(N_PAIR):
            o_ref[pl.ds(row0, 8), p * LANES:(p + 1) * LANES] = jnp.concatenate([rows[i][p] for i in range(8)], 0)

    def group(g, carry):
        f0 = pl.multiple_of(g * 8, 8)
        b0 = pl.multiple_of(tb - 8 - g * 8, 8)
        f_tiles = load8((rf_ref, vf_ref, kkf_ref, wf_ref, kdf_ref, bf_ref), f0)
        b_tiles = load8((rb_ref, vb_ref, kkb_ref, wb_ref, kdb_ref, bb_ref), b0)
        f_rows, b_rows = [None] * 8, [None] * 8
        for i in range(8):
            f_rows[i] = one_step(i, f_tiles, sf_ref)
            b_rows[7 - i] = one_step(7 - i, b_tiles, sb_ref)
        store8(of_ref, f0, f_rows)
        store8(ob_ref, b0, b_rows)
        return carry

    lax.fori_loop(0, tb // 8, group, 0)


def _rwkv_scan(r, v, kk, ws, kds, bs, *, n_ctx):
    T = r.shape[0]
    tb = 128
    nb, cb = T // tb, n_ctx // tb
    assert T % tb == 0 and n_ctx % tb == 0

    def back(j):
        return jnp.where(j < cb, cb - 1 - j, nb - 1 - (j - cb))

    fwd = pl.BlockSpec((tb, W_B), lambda j: (j, 0))
    bwd = pl.BlockSpec((tb, W_B), lambda j: (back(j), 0))
    return pl.pallas_call(
        functools.partial(_scan_kernel, tb=tb),
        grid=(nb,),
        in_specs=[fwd] * 6 + [bwd] * 6,
        out_specs=[fwd, bwd],
        out_shape=[jax.ShapeDtypeStruct((T, W_B), F32)] * 2,
        scratch_shapes=[pltpu.VMEM((N_PAIR, N_B, LANES), F32)] * 2,
        compiler_params=_cparams("arbitrary"),
        name="rwkv_scan",
    )(r, v, kk, ws[0], kds[0], bs[0], r, v, kk, ws[1], kds[1], bs[1])


def _merge_kernel(o_ref, w_ref, g_ref, out_ref, acc_ref):
    j = pl.program_id(2)
    part = jax.nn.sigmoid(g_ref[...]) * jnp.dot(o_ref[0], w_ref[0], preferred_element_type=F32)

    @pl.when(j == 0)
    def _():
        acc_ref[...] = part

    @pl.when(j > 0)
    def _():
        acc_ref[...] += part

    @pl.when(j == N_BRANCH - 1)
    def _():
        out_ref[...] = acc_ref[...].astype(out_ref.dtype)


def _merge(o3, w_branch, p, gate_off):
    _, T, W = o3.shape
    D = w_branch.shape[2]
    tm = _tile(T, (640, 512, 256, 128))
    tn = _tile(D, (2048, 1024, 512, 256, 128))
    assert gate_off % tn == 0
    gb, nj = gate_off // tn, D // tn
    return pl.pallas_call(
        _merge_kernel,
        grid=(T // tm, D // tn, N_BRANCH),
        in_specs=[pl.BlockSpec((1, tm, W), lambda i, n, j: (j, i, 0)),
                  pl.BlockSpec((1, W, tn), lambda i, n, j: (j, 0, n)),
                  pl.BlockSpec((tm, tn), lambda i, n, j: (i, gb + j * nj + n))],
        out_specs=pl.BlockSpec((tm, tn), lambda i, n, j: (i, n)),
        out_shape=jax.ShapeDtypeStruct((T, D), BF16),
        scratch_shapes=[pltpu.VMEM((tm, tn), F32)],
        compiler_params=_cparams("parallel", "parallel", "arbitrary"),
        name="merge",
    )(o3, w_branch, p)


def _router_kernel(x_ref, w_ref, b_ref, e_ref, g_ref):
    logits = jnp.dot(x_ref[...], w_ref[...], preferred_element_type=F32,
                     precision=lax.Precision.HIGHEST) + b_ref[...]
    lane = lax.broadcasted_iota(jnp.int32, logits.shape, 1)
    vals, idxs = [], []
    x = logits
    for _ in range(TOP_K):
        m = jnp.max(x, axis=-1, keepdims=True)
        idx = jnp.min(jnp.where(x == m, lane, LANES), axis=-1, keepdims=True)
        vals.append(m)
        idxs.append(idx)
        x = jnp.where(lane == idx, -jnp.inf, x)
    ex = [jnp.exp(v - vals[0]) for v in vals]
    tot = ex[0] + ex[1] + ex[2] + ex[3]
    e_out = jnp.zeros(logits.shape, jnp.int32)
    g_out = jnp.zeros(logits.shape, F32)
    for k in range(TOP_K):
        e_out = jnp.where(lane == k, idxs[k], e_out)
        g_out = jnp.where(lane == k, ex[k] / tot, g_out)
    e_ref[...] = e_out
    g_ref[...] = g_out


def _router(h, router_w, router_b):
    T, D = h.shape
    tm = _tile(T, (256, 128))
    w = jnp.zeros((D, LANES), F32).at[:, :N_EXPERTS].set(router_w)
    b = jnp.full((1, LANES), -1e30, F32).at[0, :N_EXPERTS].set(router_b)
    return pl.pallas_call(
        _router_kernel,
        grid=(T // tm,),
        in_specs=[pl.BlockSpec((tm, D), lambda i: (i, 0)),
                  pl.BlockSpec((D, LANES), lambda i: (0, 0)),
                  pl.BlockSpec((1, LANES), lambda i: (0, 0))],
        out_specs=[pl.BlockSpec((tm, LANES), lambda i: (i, 0))] * 2,
        out_shape=[jax.ShapeDtypeStruct((T, LANES), jnp.int32), jax.ShapeDtypeStruct((T, LANES), F32)],
        compiler_params=_cparams("parallel"),
        name="router_topk",
    )(h, w, b)


def _row_gather(src_ref, dst_ref, sem, index_of, n):
    def copy(r):
        return pltpu.make_async_copy(src_ref.at[pl.ds(index_of(r), 1), :], dst_ref.at[pl.ds(r, 1), :], sem)

    def start():
        for r in range(n):
            copy(r).start()

    def wait():
        for r in range(n):
            copy(r).wait()

    return start, wait


def _expert_kernel(be_ref, nu_ref, idx_ref, nidx_ref, h_ref, wgu_ref, bgu_ref, wd_ref, bd_ref, o_ref, xbuf, sem):
    i = pl.program_id(0)
    n_used = nu_ref[0]
    slot = lax.rem(i, 2)

    def gather(ids_ref, s):
        return _row_gather(h_ref, xbuf.at[s], sem.at[s], lambda r: ids_ref[0, 0, r], MOE_BLOCK)

    @pl.when(i == 0)
    def _():
        gather(idx_ref, 0)[0]()

    @pl.when(i + 1 < n_used)
    def _():
        gather(nidx_ref, 1 - slot)[0]()

    @pl.when(i < n_used)
    def _():
        gather(idx_ref, slot)[1]()
        gu = jnp.dot(xbuf[slot].astype(BF16), wgu_ref[0], preferred_element_type=F32) + bgu_ref[0]
        gate = jnp.minimum(gu[:, :D_EXPERT], SWIGLU_LIMIT)
        up = jnp.clip(gu[:, D_EXPERT:], -SWIGLU_LIMIT, SWIGLU_LIMIT)
        act = (up + 1.0) * gate * jax.nn.sigmoid(gate * SWIGLU_ALPHA)
        o_ref[...] = jnp.dot(act.astype(BF16), wd_ref[0], preferred_element_type=F32) + bd_ref[0]

    @pl.when(i >= n_used)
    def _():
        o_ref[...] = jnp.zeros_like(o_ref)


def _experts(h, row_tok, block_e, n_used, w_gu, b_gu, w_down, b_down):
    D = h.shape[1]
    n_rows = row_tok.shape[0]
    nb = n_rows // MOE_BLOCK
    ids = row_tok.reshape(nb, 1, MOE_BLOCK)
    grid_spec = pltpu.PrefetchScalarGridSpec(
        num_scalar_prefetch=2,
        grid=(nb,),
        in_specs=[pl.BlockSpec((1, 1, MOE_BLOCK), lambda i, be, nu: (i, 0, 0), memory_space=pltpu.SMEM),
                  pl.BlockSpec((1, 1, MOE_BLOCK), lambda i, be, nu: (jnp.minimum(i + 1, nb - 1), 0, 0),
                               memory_space=pltpu.SMEM),
                  pl.BlockSpec(memory_space=pl.ANY),
                  pl.BlockSpec((1, D, 2 * D_EXPERT), lambda i, be, nu: (be[i], 0, 0)),
                  pl.BlockSpec((1, 1, 2 * D_EXPERT), lambda i, be, nu: (be[i], 0, 0)),
                  pl.BlockSpec((1, D_EXPERT, D), lambda i, be, nu: (be[i], 0, 0)),
                  pl.BlockSpec((1, 1, D), lambda i, be, nu: (be[i], 0, 0))],
        out_specs=pl.BlockSpec((MOE_BLOCK, D), lambda i, be, nu: (i, 0)),
        scratch_shapes=[pltpu.VMEM((2, MOE_BLOCK, D), F32), pltpu.SemaphoreType.DMA((2,))],
    )
    return pl.pallas_call(
        _expert_kernel,
        grid_spec=grid_spec,
        out_shape=jax.ShapeDtypeStruct((n_rows, D), F32),
        compiler_params=_cparams("arbitrary"),
        name="experts",
    )(block_e, n_used, ids, ids, h, w_gu, b_gu.reshape(N_EXPERTS, 1, -1), w_down,
      b_down.reshape(N_EXPERTS, 1, -1))


def _combine_kernel(idx_ref, nidx_ref, g_ref, res_ref, mod_ref, y_ref, o_ref, buf, sem, *, ctx_rows, tm, nt):
    i = pl.program_id(0)
    slot = lax.rem(i, 2)

    def gathers(ids_ref, s):
        return [_row_gather(y_ref, buf.at[s, k], sem.at[s], functools.partial(lambda r, k: ids_ref[0, k, r], k=k), tm)
                for k in range(TOP_K)]

    @pl.when(i == 0)
    def _():
        for start, _ in gathers(idx_ref, 0):
            start()

    @pl.when(i + 1 < nt)
    def _():
        for start, _ in gathers(nidx_ref, 1 - slot):
            start()

    for _, wait in gathers(idx_ref, slot):
        wait()
    g = g_ref[...]
    y = g[:, 0:1] * buf[slot, 0]
    for k in range(1, TOP_K):
        y = y + g[:, k:k + 1] * buf[slot, k]
    rows = i * tm + lax.broadcasted_iota(jnp.int32, (tm, 1), 0)
    m = jnp.where(rows < ctx_rows, mod_ref[1:2, :], mod_ref[0:1, :])
    o_ref[...] = res_ref[...] + m * y


def _combine(yb, dest, gates, resid, mod, ctx_rows):
    T, D = resid.shape
    tm = _tile(T, (128,))
    nt = T // tm
    ids = jnp.swapaxes(dest.reshape(nt, tm, TOP_K), 1, 2)
    return pl.pallas_call(
        functools.partial(_combine_kernel, ctx_rows=ctx_rows, tm=tm, nt=nt),
        grid=(nt,),
        in_specs=[pl.BlockSpec((1, TOP_K, tm), lambda i: (i, 0, 0), memory_space=pltpu.SMEM),
                  pl.BlockSpec((1, TOP_K, tm), lambda i: (jnp.minimum(i + 1, nt - 1), 0, 0),
                               memory_space=pltpu.SMEM),
                  pl.BlockSpec((tm, LANES), lambda i: (i, 0)),
                  pl.BlockSpec((tm, D), lambda i: (i, 0)),
                  pl.BlockSpec((8, D), lambda i: (0, 0)),
                  pl.BlockSpec(memory_space=pl.ANY)],
        out_specs=pl.BlockSpec((tm, D), lambda i: (i, 0)),
        out_shape=jax.ShapeDtypeStruct((T, D), F32),
        scratch_shapes=[pltpu.VMEM((2, TOP_K, tm, D), F32), pltpu.SemaphoreType.DMA((2,))],
        compiler_params=_cparams("arbitrary"),
        name="moe_combine",
    )(ids, ids, gates, resid, mod, yb)


def _moe(h2, resid, mod, ctx_rows, router_w, router_b, w_gu, b_gu, w_down, b_down):
    T, D = h2.shape
    e128, g128 = _router(h2, router_w, router_b)
    top_e = e128[:, :TOP_K]
    onehot = (top_e[:, :, None] == jnp.arange(N_EXPERTS, dtype=jnp.int32)).astype(jnp.int32).sum(1)
    counts = onehot.sum(0)
    rank = jnp.cumsum(onehot, axis=0) - onehot
    pcounts = (counts + MOE_BLOCK - 1) // MOE_BLOCK * MOE_BLOCK
    pends = jnp.cumsum(pcounts)
    pstarts = pends - pcounts
    dest = (pstarts[top_e] + jnp.take_along_axis(rank, top_e, axis=1)).astype(jnp.int32)
    n_blocks = -(-T * TOP_K // MOE_BLOCK) + N_EXPERTS
    n_rows = n_blocks * MOE_BLOCK
    flat_dest = dest.reshape(-1)
    tok = jnp.repeat(jnp.arange(T, dtype=jnp.int32), TOP_K)
    row_tok = jnp.zeros((n_rows,), jnp.int32).at[flat_dest].set(tok)
    block_e = jnp.minimum(
        jnp.searchsorted(pends, jnp.arange(n_blocks, dtype=pends.dtype) * MOE_BLOCK, side='right'),
        N_EXPERTS - 1).astype(jnp.int32)
    n_used = (pends[-1] // MOE_BLOCK).astype(jnp.int32).reshape(1)
    yb = _experts(h2, row_tok, block_e, n_used, w_gu, b_gu, w_down, b_down)
    return _combine(yb, dest, g128, resid, mod, ctx_rows)


def _rope_tables(n_ctx, S):
    half = MLA_ROPE // 2
    inv = ROPE_THETA ** (-jnp.arange(0, half, 2, dtype=F32) / half)
    rows = jnp.repeat(jnp.arange(S // GRID_W), GRID_W).astype(F32)[:, None] * inv
    cols = jnp.tile(jnp.arange(GRID_W), S // GRID_W).astype(F32)[:, None] * inv
    ang = jnp.concatenate([rows, rows, cols, cols], -1)
    cos = jnp.concatenate([jnp.ones((n_ctx, MLA_ROPE), F32), jnp.cos(ang)], 0)
    sin = jnp.concatenate([jnp.zeros((n_ctx, MLA_ROPE), F32), jnp.sin(ang)], 0)
    first = (jnp.arange(LANES) % (MLA_ROPE // 2)) < MLA_ROPE // 4
    cos2, sin2 = (jnp.concatenate([t, t], -1) * (SWA_SCALE * LOG2E) for t in (cos, sin))
    swa_tabs = (cos2, jnp.where(first, -sin2, 0.0), jnp.where(first, 0.0, sin2))
    return cos, sin, swa_tabs


def _rope(x, cos, sin):
    q = MLA_ROPE // 4

    def rot(p):
        return jnp.concatenate([-p[..., q:], p[..., :q]], -1)

    xr = jnp.concatenate([rot(x[..., :2 * q]), rot(x[..., 2 * q:])], -1)
    return x * cos[:, None, :] + xr * sin[:, None, :]


def _pad_k(a, k):
    return jnp.pad(a, ((0, 0), (0, k - a.shape[1])))


LORA_W = 1024
O_WD_B, O_AD_F, O_AD_B, O_GD = R_W, 2 * R_W, 2 * R_W + R_A, 2 * R_W + 2 * R_A


def _seg_matrix():
    i = lax.broadcasted_iota(jnp.int32, (LANES, LANES), 0) // N_B
    j = lax.broadcasted_iota(jnp.int32, (LANES, LANES), 1) // N_B
    return (i == j).astype(BF16)


def _head_sums(x, seg):
    outs = []
    for c in range(x.shape[1] // LANES):
        rem = x[:, c * LANES:(c + 1) * LANES]
        tot = None
        for _ in range(3):
            hi = rem.astype(BF16)
            y = jnp.dot(hi, seg, preferred_element_type=F32)
            tot = y if tot is None else tot + y
            rem = rem - hi.astype(F32)
        outs.append(tot)
    return jnp.concatenate(outs, axis=1)


def _rwkv_prep_kernel(pr_ref, pk_ref, pv_ref, pl_ref, hp_r, hp_k, hp_v, hp_l, hn_r, hn_k, hn_v, hn_l,
                      mu_r, mu_k, mu_v, mu_l, kk_g, ka_g, w0_ref, a0_ref, wwf, wwb, waf, wab, wg,
                      r_o, v_o, kk_o, wf_o, kdf_o, bf_o, wb_o, kdb_o, bb_o, g_o, *, tm, n_ctx, n_tok):
    row = pl.program_id(0) * tm + lax.broadcasted_iota(jnp.int32, (tm, 1), 0)
    local = lax.broadcasted_iota(jnp.int32, (tm, 1), 0)
    has_prev = (row != 0) & (row != n_ctx)
    has_next = (row != n_ctx - 1) & (row != n_tok - 1)

    def shifted(x_ref, hp_ref, hn_ref, mu_ref):
        x = x_ref[...]
        prev = jnp.where(local == 0, hp_ref[7:8, :], pltpu.roll(x, 1, axis=0))
        nxt = jnp.where(local == tm - 1, hn_ref[0:1, :], pltpu.roll(x, tm - 1, axis=0))
        prev = jnp.where(has_prev, prev, 0.0)
        nxt = jnp.where(has_next, nxt, 0.0)
        return x + mu_ref[0:1, :] * (prev - x) + mu_ref[1:2, :] * (nxt - x)

    r = shifted(pr_ref, hp_r, hn_r, mu_r)
    k = shifted(pk_ref, hp_k, hn_k, mu_k)
    v = shifted(pv_ref, hp_v, hn_v, mu_v)
    lx = shifted(pl_ref, hp_l, hn_l, mu_l)
    seg = _seg_matrix()
    kk = k * kk_g[...]
    kk = kk * lax.rsqrt(_head_sums(kk * kk, seg) + 1e-12)
    r_o[...] = r
    v_o[...] = v
    kk_o[...] = kk
    th = jnp.tanh(lx[:, 0:2 * LANES]).astype(BF16)
    lin = lx[:, LANES:3 * LANES].astype(BF16)
    w_pre = (jnp.dot(th[:, 0:LANES], wwf[...], preferred_element_type=F32),
             jnp.dot(th, wwb[...], preferred_element_type=F32))
    a_pre = (jnp.dot(lin, waf[...], preferred_element_type=F32),
             jnp.dot(lin[:, LANES:], wab[...], preferred_element_type=F32))
    for d, (w_o, kd_o, b_o) in enumerate(((wf_o, kdf_o, bf_o), (wb_o, kdb_o, bb_o))):
        wp = w0_ref[d:d + 1, :] + w_pre[d]
        w_o[...] = jnp.exp(-jnp.exp(-jax.nn.softplus(-wp) - 0.5))
        a = jax.nn.sigmoid(a0_ref[d:d + 1, :] + a_pre[d])
        kd_o[...] = k * (1.0 + (a - 1.0) * ka_g[...])
        b_o[...] = kk * a
    gd = jax.nn.sigmoid(lx[:, O_GD:O_GD + R_G]).astype(BF16)
    g_o[...] = jnp.dot(gd, wg[...], preferred_element_type=F32)


def _rows8(a):
    return jnp.pad(a, ((0, 8 - a.shape[0]), (0, 0))).astype(F32)


def _rwkv_prep(p, lo, lp, n_ctx):
    T = p.shape[0]
    tm = 128
    assert T % tm == 0 and n_ctx % tm == 0
    hb = tm // 8
    last8 = T // 8 - 1
    main = lambda w, off: pl.BlockSpec((tm, w), lambda i: (i, off // w))
    halo_p = lambda w, off: pl.BlockSpec((8, w), lambda i: (jnp.maximum(i * hb - 1, 0), off // w))
    halo_n = lambda w, off: pl.BlockSpec((8, w), lambda i: (jnp.minimum((i + 1) * hb, last8), off // w))
    groups = ((W_B, lo['r']), (W_B, lo['k']), (W_B, lo['v']), (LORA_W, lo['lora']))
    for w, off in groups:
        assert off % w == 0
    full = lambda a: pl.BlockSpec(a.shape, lambda i: (0, 0))
    mu = lp['rwkv_mu']
    mus = [_rows8(mu[:, 0:W_B]), _rows8(mu[:, W_B:2 * W_B]), _rows8(mu[:, 2 * W_B:3 * W_B]),
           _rows8(jnp.pad(mu[:, 3 * W_B:], ((0, 0), (0, LORA_W - (C_RWKV - 3 * W_B)))))]

    def slot(w, lo_row, n_rows):
        return jnp.pad(w, ((lo_row, n_rows - lo_row - w.shape[0]), (0, 0))).astype(BF16)

    params = mus + [lp['rwkv_k_k'].reshape(1, W_B), lp['rwkv_k_a'].reshape(1, W_B),
                    _rows8(lp['rwkv_w0']), _rows8(lp['rwkv_a0']),
                    slot(lp['rwkv_w_up'][0], 0, LANES), slot(lp['rwkv_w_up'][1], O_WD_B, 2 * LANES),
                    slot(lp['rwkv_a_up'][0], O_AD_F - LANES, 2 * LANES),
                    slot(lp['rwkv_a_up'][1], O_AD_B - 2 * LANES, LANES),
                    lp['rwkv_g_up'].astype(BF16)]
    in_specs = ([main(w, off) for w, off in groups] + [halo_p(w, off) for w, off in groups]
                + [halo_n(w, off) for w, off in groups] + [full(a) for a in params])
    outs = pl.pallas_call(
        functools.partial(_rwkv_prep_kernel, tm=tm, n_ctx=n_ctx, n_tok=T),
        grid=(T // tm,),
        in_specs=in_specs,
        out_specs=[pl.BlockSpec((tm, W_B), lambda i: (i, 0))] * 10,
        out_shape=[jax.ShapeDtypeStruct((T, W_B), F32)] * 10,
        compiler_params=_cparams("parallel"),
        name="rwkv_prep",
    )(*([p] * 12 + params))
    return outs


def _rwkv_out_kernel(of_ref, ob_ref, r_ref, v_ref, kdf_ref, kdb_ref, g_ref, lng_ref, lnb_ref, rk_ref, o_ref):
    seg = _seg_matrix()
    o = of_ref[...] + ob_ref[...]
    d = o - _head_sums(o, seg) * (1.0 / N_B)
    var = _head_sums(d * d, seg) * (1.0 / N_B)
    on = d * lax.rsqrt(var + GN_EPS) * lng_ref[...] + lnb_ref[...]
    coef = _head_sums(r_ref[...] * (kdf_ref[...] + kdb_ref[...]) * rk_ref[...], seg)
    o_ref[...] = ((on + coef * v_ref[...]) * g_ref[...]).astype(o_ref.dtype)


def _rwkv_out(o_f, o_b, r, v, kd_f, kd_b, g, lp):
    T = r.shape[0]
    tm = _tile(T, (128,))
    blk = pl.BlockSpec((tm, W_B), lambda i: (i, 0))
    par = pl.BlockSpec((1, W_B), lambda i: (0, 0))
    return pl.pallas_call(
        _rwkv_out_kernel,
        grid=(T // tm,),
        in_specs=[blk] * 7 + [par] * 3,
        out_specs=blk,
        out_shape=jax.ShapeDtypeStruct((T, W_B), BF16),
        compiler_params=_cparams("parallel"),
        name="rwkv_out",
    )(o_f, o_b, r, v, kd_f, kd_b, g, lp['rwkv_ln_g'].reshape(1, W_B), lp['rwkv_ln_b'].reshape(1, W_B),
      lp['rwkv_r_k'].reshape(1, W_B))


def _rwkv_mixer(p, lo, lp, n_ctx):
    r, v, kk, w_f, kd_f, b_f, w_b, kd_b, b_b, g = _rwkv_prep(p, lo, lp, n_ctx)
    o_f, o_b = _rwkv_scan(r, v, kk, (w_f, w_b), (kd_f, kd_b), (b_f, b_b), n_ctx=n_ctx)
    return _rwkv_out(o_f, o_b, r, v, kd_f, kd_b, g, lp)


def _in_layout(D):
    o = {}
    o['g'] = 0
    o['r'] = N_BRANCH * D
    o['k'] = o['r'] + W_B
    o['v'] = o['k'] + W_B
    o['sq'] = o['v'] + W_B
    o['q'] = o['sq'] + W_C
    o['lora'] = o['q'] + Q_LORA
    o['kv'] = o['lora'] + LORA_W
    o['sk'] = o['kv'] + KV_LORA
    o['sv'] = o['sk'] + KV_C * HD_C
    o['kr'] = o['sv'] + KV_C * HD_C
    o['end'] = -(-(o['kr'] + MLA_ROPE) // 1024) * 1024
    return o


def _prep_w_in(w_in, D):
    s = [int(v) for v in np.cumsum([0, Q_LORA, KV_LORA, MLA_ROPE, C_RWKV, W_C, KV_C * HD_C, KV_C * HD_C,
                                    N_BRANCH * D])]
    seg = lambda i: w_in[:, s[i]:s[i + 1]]
    rw = seg(3)
    zeros = lambda n: jnp.zeros((w_in.shape[0], n), w_in.dtype)
    lo = _in_layout(D)
    cols = [seg(7), rw[:, :3 * W_B], seg(4), seg(0), rw[:, 3 * W_B:], zeros(LORA_W - (C_RWKV - 3 * W_B)), seg(1),
            seg(5), seg(6), seg(2), zeros(lo['end'] - lo['kr'] - MLA_ROPE)]
    return jnp.concatenate(cols, -1).astype(BF16)


def _rot_cols(w):
    q = MLA_ROPE // 4
    return jnp.concatenate([-w[..., q:2 * q], w[..., :q], -w[..., 3 * q:], w[..., 2 * q:3 * q]], -1)


def _prep_wq_up(wq_up):
    w = wq_up.reshape(Q_LORA, H_A, MLA_NOPE + MLA_ROPE)
    w = jnp.concatenate([w, _rot_cols(w[..., MLA_NOPE:])], -1)
    return w.reshape(Q_LORA, H_A * MLA_QW).astype(BF16)


def _layer(xall, mod, lp, rope, n_ctx):
    T, D = xall.shape
    S = T - n_ctx
    cos, sin, swa_tabs = rope
    md = [mod[:, j * D:(j + 1) * D] for j in range(6)]
    lo = _in_layout(D)

    h1 = _rownorm(xall, lp['norm1_g'], shift=md[0], scale=md[1], ctx_rows=n_ctx, name="norm1")
    w_in = _prep_w_in(lp['w_in'], D)
    p = _matmul(h1, w_in, tm=_tile(T, (1280, 640, 256, 128)), tn=512, name="w_in")

    qn = _rownorm(p, lp['mla_qn_g'], col_off=lo['q'], width=Q_LORA, name="mla_qnorm")
    q_tab = jnp.concatenate([jnp.ones((T, MLA_NOPE), F32), cos, sin], -1) * (MLA_SCALE * LOG2E)
    q = _matmul(qn, _prep_wq_up(lp['mla_wq_up']), row_table=q_tab, out_dtype=BF16, name="mla_q_up")
    kvn = _rownorm(p, lp['mla_kvn_g'], col_off=lo['kv'], width=KV_LORA, name="mla_kvnorm")
    kv = _matmul(kvn, lp['mla_wkv_up'].astype(BF16), out_dtype=BF16, name="mla_kv_up")
    kr = _rope(p[:, None, lo['kr']:lo['kr'] + MLA_ROPE], cos, sin)[:, 0]
    kr = jnp.concatenate([kr, kr], -1).astype(BF16)
    oa_x = _mla_attention(q[n_ctx:], kv, kr, n_q=S, n_keys=T, name="mla_latent")
    oa_c = _mla_attention(q[:n_ctx], kv, kr, n_q=n_ctx, n_keys=n_ctx, name="mla_context")
    o_a = jnp.concatenate([oa_c, oa_x], 0)

    o_b = _rwkv_mixer(p, lo, lp, n_ctx)

    kvw = KV_C * HD_C
    kc = _rope(p[:, lo['sk']:lo['sk'] + kvw].reshape(T, KV_C, HD_C), cos, sin).reshape(T, kvw)
    vc = p[:, lo['sv']:lo['sv'] + kvw]
    sink_rows = jnp.repeat(lp['swa_sink'].astype(F32).reshape(KV_C, G_C) * LOG2E, 128, axis=1)
    sink_rows = sink_rows.reshape(KV_C, G_C * 128, 1)
    oc_x = _swa_attention(p, lo['sq'], swa_tabs, kc, vc, sink_rows, n_ctx=n_ctx, band=True, name="swa_latent")
    oc_c = _swa_attention(p, lo['sq'], swa_tabs, kc, vc, sink_rows, n_ctx=n_ctx, band=False, name="swa_context")
    o_c = jnp.concatenate([oc_c, oc_x], 0)

    m = _merge(jnp.stack([o_a, o_b, o_c], 0), lp['w_branch'].astype(BF16), p, lo['g'])
    x1 = _matmul(m, lp['w_out'].astype(BF16), resid=xall, mod=md[2], ctx_rows=n_ctx, name="w_out")
    h2 = _rownorm(x1, lp['norm2_g'], shift=md[3], scale=md[4], ctx_rows=n_ctx, out_dtype=F32, name="norm2")
    return _moe(h2, x1, md[5], n_ctx, lp['router_w'], lp['router_b'], lp['exp_w_gu'].astype(BF16),
                lp['exp_b_gu'], lp['exp_w_down'].astype(BF16), lp['exp_b_down'])


def kernel(x, c, ctx, c_ctx, ada_w, ada_b, norm1_g, w_in, mla_qn_g, mla_kvn_g, mla_wq_up, mla_wkv_up, rwkv_mu, rwkv_w0, rwkv_w_up, rwkv_a0, rwkv_a_up, rwkv_g_up, rwkv_k_k, rwkv_k_a, rwkv_r_k, rwkv_ln_g, rwkv_ln_b, swa_sink, w_branch, w_out, norm2_g, router_w, router_b, exp_w_gu, exp_b_gu, exp_w_down, exp_b_down, final_g):
    B, S, D = x.shape
    n_ctx = ctx.shape[1]
    assert B == 1 and c.shape[0] == 1
    depth = ada_w.shape[0]
    rope = _rope_tables(n_ctx, S)
    xall = jnp.concatenate([ctx[0], x[0]], 0)
    cond = jax.nn.silu(jnp.concatenate([c, c_ctx[None], jnp.zeros((6, D), F32)], 0))
    for i in range(depth):
        lp = dict(norm1_g=norm1_g[i], w_in=w_in[i],
                  mla_qn_g=mla_qn_g[i], mla_kvn_g=mla_kvn_g[i], mla_wq_up=mla_wq_up[i], mla_wkv_up=mla_wkv_up[i],
                  rwkv_mu=rwkv_mu[i], rwkv_w0=rwkv_w0[i], rwkv_w_up=rwkv_w_up[i], rwkv_a0=rwkv_a0[i],
                  rwkv_a_up=rwkv_a_up[i], rwkv_g_up=rwkv_g_up[i], rwkv_k_k=rwkv_k_k[i], rwkv_k_a=rwkv_k_a[i],
                  rwkv_r_k=rwkv_r_k[i], rwkv_ln_g=rwkv_ln_g[i], rwkv_ln_b=rwkv_ln_b[i], swa_sink=swa_sink[i],
                  w_branch=w_branch[i], w_out=w_out[i], norm2_g=norm2_g[i], router_w=router_w[i],
                  router_b=router_b[i], exp_w_gu=exp_w_gu[i], exp_b_gu=exp_b_gu[i],
                  exp_w_down=exp_w_down[i], exp_b_down=exp_b_down[i])
        mod = _matmul(cond, ada_w, b_slab=i, bias=ada_b[i], tm=8, tn=1024, name="adaln")
        xall = _layer(xall, mod, lp, rope, n_ctx)
    out = _rownorm(xall, final_g, row_off=n_ctx, rows=S, out_dtype=F32, name="final_norm")
    return out.reshape(B, S, D)
```

```python
import functools

import numpy as np
import jax
import jax.numpy as jnp
from jax import lax
from jax.experimental import pallas as pl
from jax.experimental.pallas import tpu as pltpu

F32 = jnp.float32
BF16 = jnp.bfloat16

GRID_W = 64
ROPE_THETA = 10000.0
NORM_EPS = 1e-6
H_A, MLA_NOPE, MLA_ROPE, MLA_V = 16, 128, 64, 128
Q_LORA, KV_LORA = 1024, 512
MLA_SCALE = (MLA_NOPE + MLA_ROPE) ** -0.5
LOG2E = 1.4426950408889634
MLA_QW = 256
H_B, N_B = 32, 64
W_B = H_B * N_B
R_W, R_A, R_G = 96, 96, 256
GN_EPS = 64e-5
H_C, KV_C, HD_C = 32, 4, 64
G_C = H_C // KV_C
WINDOW = 128
W_C = H_C * HD_C
SWA_SCALE = HD_C ** -0.5
N_BRANCH = 3
N_EXPERTS, TOP_K, D_EXPERT = 32, 4, 512
SWIGLU_LIMIT, SWIGLU_ALPHA = 7.0, 1.702
MOE_BLOCK = 128
C_RWKV = 3 * W_B + 2 * R_W + 2 * R_A + R_G

LANES = 128
VMEM_LIMIT = 56 * 1024 * 1024


def _cparams(*sem):
    return pltpu.CompilerParams(dimension_semantics=sem, vmem_limit_bytes=VMEM_LIMIT)


def _lane_tile(x, k):
    return x if k == 1 else jnp.concatenate([x] * k, axis=1)


def _tile(n, prefs):
    for t in prefs:
        if n % t == 0:
            return t
    return n


def _mm_kernel(*refs, nk, has_bias, resid, has_tab, ctx_rows, tm, precision):
    it = iter(refs)
    a_ref, b_ref = next(it), next(it)
    bias_ref = next(it) if has_bias else None
    res_ref, mod_ref = (next(it), next(it)) if resid else (None, None)
    tab_ref = next(it) if has_tab else None
    o_ref = next(it)
    acc_ref = next(it) if nk > 1 else None

    if precision is None:
        part = jnp.dot(a_ref[...].astype(BF16), b_ref[...].astype(BF16), preferred_element_type=F32)
    else:
        part = jnp.dot(a_ref[...], b_ref[...], preferred_element_type=F32, precision=precision)

    def finish(acc):
        if has_bias:
            acc = acc + bias_ref[...]
        if resid:
            rows = pl.program_id(0) * tm + lax.broadcasted_iota(jnp.int32, (tm, 1), 0)
            m = jnp.where(rows < ctx_rows, mod_ref[1:2, :], mod_ref[0:1, :])
            acc = res_ref[...] + m * acc
        if has_tab:
            acc = acc * _lane_tile(tab_ref[...], acc.shape[1] // tab_ref.shape[1])
        o_ref[...] = acc.astype(o_ref.dtype)

    if nk == 1:
        finish(part)
    else:
        k = pl.program_id(2)

        @pl.when(k == 0)
        def _():
            acc_ref[...] = part

        @pl.when(k > 0)
        def _():
            acc_ref[...] += part

        @pl.when(k == nk - 1)
        def _():
            finish(acc_ref[...])


def _matmul(a, b, *, a_col_off=0, k_dim=None, bias=None, resid=None, mod=None, ctx_rows=0, row_table=None,
            b_slab=0, out_dtype=F32, tm=None, tn=None, tk=None, precision=None, name="matmul"):
    M = a.shape[0]
    K, N = b.shape[-2:]
    if k_dim is None:
        assert a.shape[1] == K and a_col_off == 0
    tm = tm or _tile(M, (640, 512, 256, 128))
    tn = tn or _tile(N, (1024, 896, 512, 256, 128))
    tk = tk or K
    assert M % tm == 0 and N % tn == 0 and K % tk == 0 and a_col_off % tk == 0
    nk = K // tk
    koff = a_col_off // tk
    if b.ndim == 3:
        b_spec = pl.BlockSpec((None, tk, tn), lambda i, j, k: (b_slab, k, j))
    else:
        b_spec = pl.BlockSpec((tk, tn), lambda i, j, k: (k, j))
    in_specs = [pl.BlockSpec((tm, tk), lambda i, j, k: (i, k + koff)), b_spec]
    args = [a, b]
    if bias is not None:
        in_specs.append(pl.BlockSpec((1, tn), lambda i, j, k: (0, j)))
        args.append(bias.reshape(1, N).astype(F32))
    if resid is not None:
        in_specs.append(pl.BlockSpec((tm, tn), lambda i, j, k: (i, j)))
        in_specs.append(pl.BlockSpec((8, tn), lambda i, j, k: (0, j)))
        args += [resid, mod]
    if row_table is not None:
        assert tn % row_table.shape[1] == 0
        in_specs.append(pl.BlockSpec((tm, row_table.shape[1]), lambda i, j, k: (i, 0)))
        args.append(row_table)
    kern = functools.partial(_mm_kernel, nk=nk, has_bias=bias is not None, resid=resid is not None,
                             has_tab=row_table is not None, ctx_rows=ctx_rows, tm=tm, precision=precision)
    return pl.pallas_call(
        kern,
        grid=(M // tm, N // tn, nk),
        in_specs=in_specs,
        out_specs=pl.BlockSpec((tm, tn), lambda i, j, k: (i, j)),
        out_shape=jax.ShapeDtypeStruct((M, N), out_dtype),
        scratch_shapes=[pltpu.VMEM((tm, tn), F32)] if nk > 1 else [],
        compiler_params=_cparams("parallel", "parallel", "arbitrary"),
        name=name,
    )(*args)


def _norm_kernel(*refs, eps, modulated, ctx_rows, tm):
    if modulated:
        x_ref, g_ref, sh_ref, sc_ref, o_ref = refs
    else:
        x_ref, g_ref, o_ref = refs
    x = x_ref[...].astype(F32)
    y = x * lax.rsqrt(jnp.mean(x * x, axis=-1, keepdims=True) + eps) * g_ref[...]
    if modulated:
        rows = pl.program_id(0) * tm + lax.broadcasted_iota(jnp.int32, (tm, 1), 0)
        isctx = rows < ctx_rows
        sc = jnp.where(isctx, sc_ref[1:2, :], sc_ref[0:1, :])
        sh = jnp.where(isctx, sh_ref[1:2, :], sh_ref[0:1, :])
        y = y * (1.0 + sc) + sh
    o_ref[...] = y.astype(o_ref.dtype)


def _rownorm(x, g, *, col_off=0, width=None, row_off=0, rows=None, shift=None, scale=None,
             ctx_rows=0, out_dtype=BF16, eps=NORM_EPS, name="rownorm"):
    width = width or x.shape[1]
    rows = rows or x.shape[0]
    tm = _tile(rows, (256, 128))
    assert col_off % width == 0 and row_off % tm == 0 and rows % tm == 0
    cb, rb = col_off // width, row_off // tm
    modulated = shift is not None
    in_specs = [pl.BlockSpec((tm, width), lambda i: (i + rb, cb)),
                pl.BlockSpec((1, width), lambda i: (0, 0))]
    args = [x, g.reshape(1, width).astype(F32)]
    if modulated:
        in_specs += [pl.BlockSpec((8, width), lambda i: (0, 0))] * 2
        args += [shift, scale]
    kern = functools.partial(_norm_kernel, eps=eps, modulated=modulated, ctx_rows=ctx_rows, tm=tm)
    return pl.pallas_call(
        kern,
        grid=(rows // tm,),
        in_specs=in_specs,
        out_specs=pl.BlockSpec((tm, width), lambda i: (i, 0)),
        out_shape=jax.ShapeDtypeStruct((rows, width), out_dtype),
        compiler_params=_cparams("parallel"),
        name=name,
    )(*args)


def _flash_kernel(q_ref, kn_ref, kr_ref, v_ref, o_ref, m_ref, acc_ref, *, nk, sub):
    kj = pl.program_id(2)

    @pl.when(kj == 0)
    def _():
        m_ref[...] = jnp.full_like(m_ref, -jnp.inf)
        acc_ref[...] = jnp.zeros_like(acc_ref)

    q = q_ref[...]
    m, acc = m_ref[...], acc_ref[...]
    ones = jnp.ones((sub, MLA_V), BF16)
    for c in range(kn_ref.shape[0] // sub):
        rows = slice(c * sub, (c + 1) * sub)
        k = jnp.concatenate([kn_ref[rows, :], kr_ref[rows, :]], axis=-1)
        s = lax.dot_general(q, k, (((1,), (1,)), ((), ())), preferred_element_type=F32)
        m_new = jnp.maximum(m, jnp.max(s, axis=-1, keepdims=True))
        p = jnp.exp2(s - _lane_tile(m_new, sub // LANES)).astype(BF16)
        v1 = jnp.concatenate([v_ref[rows, :], ones], axis=-1)
        acc = _lane_tile(jnp.exp2(m - m_new), 2) * acc + jnp.dot(p, v1, preferred_element_type=F32)
        m = m_new
    m_ref[...] = m
    acc_ref[...] = acc

    @pl.when(kj == nk - 1)
    def _():
        acc = acc_ref[...]
        o_ref[...] = (acc[:, :MLA_V] / acc[:, MLA_V:]).astype(o_ref.dtype)


def _mla_attention(q, kv, kr, *, n_q, n_keys, name):
    tq = _tile(n_q, (2048, 1024, 512, 256, 128))
    tk = _tile(n_keys, (3328, 1280, 640, 512, 256, 128))
    nk = n_keys // tk
    sub = _tile(tk, (256, 128))
    return pl.pallas_call(
        functools.partial(_flash_kernel, nk=nk, sub=sub),
        grid=(H_A, n_q // tq, nk),
        in_specs=[pl.BlockSpec((tq, MLA_QW), lambda h, i, j: (i, h)),
                  pl.BlockSpec((tk, MLA_NOPE), lambda h, i, j: (j, 2 * h)),
                  pl.BlockSpec((tk, LANES), lambda h, i, j: (j, 0)),
                  pl.BlockSpec((tk, MLA_V), lambda h, i, j: (j, 2 * h + 1))],
        out_specs=pl.BlockSpec((tq, MLA_V), lambda h, i, j: (i, h)),
        out_shape=jax.ShapeDtypeStruct((n_q, H_A * MLA_V), BF16),
        scratch_shapes=[pltpu.VMEM((tq, LANES), F32), pltpu.VMEM((tq, 2 * MLA_V), F32)],
        compiler_params=_cparams("parallel", "parallel", "arbitrary"),
        name=name,
    )(q, kv, kr, kv)


def _swa_kernel(*refs, band, nb):
    if band:
        (q_ref, c_ref, s1_ref, s2_ref, sink_ref, kc_ref, vc_ref, kp_ref, k0_ref, kn_ref, vp_ref, v0_ref, vn_ref,
         o_ref) = refs
    else:
        q_ref, c_ref, s1_ref, s2_ref, sink_ref, kc_ref, vc_ref, o_ref = refs
    i = pl.program_id(0)
    qb = q_ref.shape[0]
    cos, s1, s2 = c_ref[...], s1_ref[...], s2_ref[...]

    def roped(x):
        return (x * cos + pltpu.roll(x, LANES - HD_C // 4, axis=1) * s1 + pltpu.roll(x, HD_C // 4, axis=1) * s2)

    lane = lax.broadcasted_iota(jnp.int32, (1, LANES), 1)
    low = lane < HD_C
    if band:
        r = lax.broadcasted_iota(jnp.int32, (qb, qb), 0)
        c = lax.broadcasted_iota(jnp.int32, (qb, qb), 1)
        ok_prev = c - r + jnp.where(i > 0, 0, -4 * qb) >= qb - WINDOW
        ok_next = r - c + jnp.where(i < nb - 1, 0, -4 * qb) >= qb - WINDOW

    def masked(s, ok):
        return jnp.where(ok[None], s.reshape(G_C, qb, qb), -jnp.inf).reshape(G_C * qb, qb)

    for g in range(KV_C):
        grp, upper = g // 2, g % 2 == 1
        mine = ~low if upper else low

        def both_halves(x_ref):
            x = x_ref[:, grp * LANES:(grp + 1) * LANES]
            return jnp.where(mine, x, pltpu.roll(x, HD_C, axis=1))

        qs = []
        for cgrp in range(G_C // 2):
            col = (g * (G_C // 2) + cgrp) * LANES
            x = roped(q_ref[:, col:col + LANES])
            qs += [jnp.where(low, x, 0.0), jnp.where(low, 0.0, x)]
        qs = jnp.concatenate(qs, axis=0).astype(BF16)

        def scores(k_ref):
            return lax.dot_general(qs, both_halves(k_ref).astype(BF16), (((1,), (1,)), ((), ())),
                                   preferred_element_type=F32)

        parts = [(scores(kc_ref), vc_ref)]
        if band:
            parts += [(masked(scores(kp_ref), ok_prev), vp_ref), (scores(k0_ref), v0_ref),
                      (masked(scores(kn_ref), ok_next), vn_ref)]
        sink = sink_ref[g]
        m = sink
        for s, _ in parts:
            m = jnp.maximum(m, jnp.max(s, axis=-1, keepdims=True))
        l_sink = jnp.exp2(sink - m)
        acc = [jnp.zeros((qb, LANES), F32)] * G_C
        for s, v_ref in parts:
            p = jnp.exp2(s - _lane_tile(m, s.shape[1] // LANES)).astype(BF16)
            v2 = both_halves(v_ref)
            v_half = (jnp.where(low, v2, 1.0).astype(BF16), jnp.where(low, 1.0, v2).astype(BF16))
            acc = [acc[j] + jnp.dot(p[j * qb:(j + 1) * qb], v_half[j % 2], preferred_element_type=F32)
                   for j in range(G_C)]
        for cgrp in range(G_C // 2):
            a_lo, a_hi = acc[2 * cgrp], acc[2 * cgrp + 1]
            lo_rows = slice((2 * cgrp) * qb, (2 * cgrp + 1) * qb)
            hi_rows = slice((2 * cgrp + 1) * qb, (2 * cgrp + 2) * qb)
            den = jnp.where(low, pltpu.roll(a_lo, HD_C, axis=1) + l_sink[lo_rows],
                            pltpu.roll(a_hi, HD_C, axis=1) + l_sink[hi_rows])
            tile = jnp.where(low, a_lo, a_hi) / den
            col = (g * (G_C // 2) + cgrp) * LANES
            o_ref[:, col:col + LANES] = tile.astype(o_ref.dtype)


def _swa_attention(p, q_off, tabs, kr, vv, sink_rows, *, n_ctx, band, name):
    T = p.shape[0]
    qb = 128
    assert WINDOW == qb and n_ctx % qb == 0 and q_off % W_C == 0
    cb = n_ctx // qb
    kvw = KV_C * HD_C
    nb, row0 = ((T - n_ctx) // qb, cb) if band else (cb, 0)
    tab = pl.BlockSpec((qb, LANES), lambda i: (i + row0, 0))
    in_specs = [pl.BlockSpec((qb, W_C), lambda i: (i + row0, q_off // W_C)), tab, tab, tab,
                pl.BlockSpec((KV_C, G_C * qb, LANES), lambda i: (0, 0, 0)),
                pl.BlockSpec((n_ctx, kvw), lambda i: (0, 0)),
                pl.BlockSpec((n_ctx, kvw), lambda i: (0, 0))]
    args = [p, *tabs, sink_rows, kr, vv]
    if band:
        blk = lambda off: pl.BlockSpec((qb, kvw), lambda i: (cb + jnp.clip(i + off, 0, nb - 1), 0))
        in_specs += [blk(-1), blk(0), blk(1)] * 2
        args += [kr, kr, kr, vv, vv, vv]
    return pl.pallas_call(
        functools.partial(_swa_kernel, band=band, nb=nb),
        grid=(nb,),
        in_specs=in_specs,
        out_specs=pl.BlockSpec((qb, W_C), lambda i: (i, 0)),
        out_shape=jax.ShapeDtypeStruct((nb * qb, W_C), BF16),
        compiler_params=_cparams("parallel"),
        name=name,
    )(*args)


N_PAIR = H_B // 2


def _scan_kernel(rf_ref, vf_ref, kkf_ref, wf_ref, kdf_ref, bf_ref, rb_ref, vb_ref, kkb_ref, wb_ref, kdb_ref,
                 bb_ref, of_ref, ob_ref, sf_ref, sb_ref, *, tb):
    @pl.when(pl.program_id(0) == 0)
    def _():
        sf_ref[...] = jnp.zeros_like(sf_ref)
        sb_ref[...] = jnp.zeros_like(sb_ref)

    lane_r = lax.broadcasted_iota(jnp.int32, (LANES, LANES), 0)
    lane_c = lax.broadcasted_iota(jnp.int32, (LANES, LANES), 1)
    seg = (lane_r // N_B == lane_c // N_B).astype(BF16)
    on_diag = (lax.broadcasted_iota(jnp.int32, (N_B, LANES), 0)
               == lax.broadcasted_iota(jnp.int32, (N_B, LANES), 1) % N_B)
    diag = on_diag.astype(F32)
    diag16 = on_diag.astype(BF16)

    def segsum(x):
        return jnp.dot(x, seg, preferred_element_type=F32)

    def tile_of(x, p):
        return x[p * N_B:(p + 1) * N_B, :]

    def one_step(i, tiles, s_ref):
        r_t, v_t, kk_t, w_t, kd_t, b_t = ([x[i:i + 1, :] for x in op] for op in tiles)
        sa = segsum(jnp.concatenate([(s_ref[p] * kk_t[p]).astype(BF16) for p in range(N_PAIR)], axis=0))
        vcol = segsum(jnp.concatenate(
            [diag16 * jnp.broadcast_to(v_t[p], (N_B, LANES)).astype(BF16) for p in range(N_PAIR)], axis=0))
        new = []
        for p in range(N_PAIR):
            s = s_ref[p] * w_t[p] - tile_of(sa, p) * b_t[p] + tile_of(vcol, p) * kd_t[p]
            s_ref[p] = s
            new.append((s * r_t[p]).astype(BF16))
        oc = segsum(jnp.concatenate(new, axis=0))
        return [jnp.sum(diag * tile_of(oc, p), axis=0, keepdims=True) for p in range(N_PAIR)]

    def load8(refs, row0):
        return [[ref[pl.ds(row0, 8), p * LANES:(p + 1) * LANES] for p in range(N_PAIR)] for ref in refs]

    def store8(o_ref, row0, rows):
        for p in range(N_PAIR):
            o_ref[pl.ds(row0, 8), p * LANES:(p + 1) * LANES] = jnp.concatenate([rows[i][p] for i in range(8)], 0)

    def group(g, carry):
        f0 = pl.multiple_of(g * 8, 8)
        b0 = pl.multiple_of(tb - 8 - g * 8, 8)
        f_tiles = load8((rf_ref, vf_ref, kkf_ref, wf_ref, kdf_ref, bf_ref), f0)
        b_tiles = load8((rb_ref, vb_ref, kkb_ref, wb_ref, kdb_ref, bb_ref), b0)
        f_rows, b_rows = [None] * 8, [None] * 8
        for i in range(8):
            f_rows[i] = one_step(i, f_tiles, sf_ref)
            b_rows[7 - i] = one_step(7 - i, b_tiles, sb_ref)
        store8(of_ref, f0, f_rows)
        store8(ob_ref, b0, b_rows)
        return carry

    lax.fori_loop(0, tb // 8, group, 0)


def _rwkv_scan(r, v, kk, ws, kds, bs, *, n_ctx):
    T = r.shape[0]
    tb = 128
    nb, cb = T // tb, n_ctx // tb
    assert T % tb == 0 and n_ctx % tb == 0

    def back(j):
        return jnp.where(j < cb, cb - 1 - j, nb - 1 - (j - cb))

    fwd = pl.BlockSpec((tb, W_B), lambda j: (j, 0))
    bwd = pl.BlockSpec((tb, W_B), lambda j: (back(j), 0))
    return pl.pallas_call(
        functools.partial(_scan_kernel, tb=tb),
        grid=(nb,),
        in_specs=[fwd] * 6 + [bwd] * 6,
        out_specs=[fwd, bwd],
        out_shape=[jax.ShapeDtypeStruct((T, W_B), F32)] * 2,
        scratch_shapes=[pltpu.VMEM((N_PAIR, N_B, LANES), F32)] * 2,
        compiler_params=_cparams("arbitrary"),
        name="rwkv_scan",
    )(r, v, kk, ws[0], kds[0], bs[0], r, v, kk, ws[1], kds[1], bs[1])


def _merge_kernel(o_ref, w_ref, g_ref, out_ref, acc_ref):
    j = pl.program_id(2)
    part = jax.nn.sigmoid(g_ref[...]) * jnp.dot(o_ref[0], w_ref[0], preferred_element_type=F32)

    @pl.when(j == 0)
    def _():
        acc_ref[...] = part

    @pl.when(j > 0)
    def _():
        acc_ref[...] += part

    @pl.when(j == N_BRANCH - 1)
    def _():
        out_ref[...] = acc_ref[...].astype(out_ref.dtype)


def _merge(o3, w_branch, p, gate_off):
    _, T, W = o3.shape
    D = w_branch.shape[2]
    tm = _tile(T, (640, 512, 256, 128))
    tn = _tile(D, (2048, 1024, 512, 256, 128))
    assert gate_off % tn == 0
    gb, nj = gate_off // tn, D // tn
    return pl.pallas_call(
        _merge_kernel,
        grid=(T // tm, D // tn, N_BRANCH),
        in_specs=[pl.BlockSpec((1, tm, W), lambda i, n, j: (j, i, 0)),
                  pl.BlockSpec((1, W, tn), lambda i, n, j: (j, 0, n)),
                  pl.BlockSpec((tm, tn), lambda i, n, j: (i, gb + j * nj + n))],
        out_specs=pl.BlockSpec((tm, tn), lambda i, n, j: (i, n)),
        out_shape=jax.ShapeDtypeStruct((T, D), BF16),
        scratch_shapes=[pltpu.VMEM((tm, tn), F32)],
        compiler_params=_cparams("parallel", "parallel", "arbitrary"),
        name="merge",
    )(o3, w_branch, p)


def _router_kernel(x_ref, w_ref, b_ref, e_ref, g_ref):
    logits = jnp.dot(x_ref[...], w_ref[...], preferred_element_type=F32,
                     precision=lax.Precision.HIGHEST) + b_ref[...]
    lane = lax.broadcasted_iota(jnp.int32, logits.shape, 1)
    vals, idxs = [], []
    x = logits
    for _ in range(TOP_K):
        m = jnp.max(x, axis=-1, keepdims=True)
        idx = jnp.min(jnp.where(x == m, lane, LANES), axis=-1, keepdims=True)
        vals.append(m)
        idxs.append(idx)
        x = jnp.where(lane == idx, -jnp.inf, x)
    ex = [jnp.exp(v - vals[0]) for v in vals]
    tot = ex[0] + ex[1] + ex[2] + ex[3]
    e_out = jnp.zeros(logits.shape, jnp.int32)
    g_out = jnp.zeros(logits.shape, F32)
    for k in range(TOP_K):
        e_out = jnp.where(lane == k, idxs[k], e_out)
        g_out = jnp.where(lane == k, ex[k] / tot, g_out)
    e_ref[...] = e_out
    g_ref[...] = g_out


def _router(h, router_w, router_b):
    T, D = h.shape
    tm = _tile(T, (256, 128))
    w = jnp.zeros((D, LANES), F32).at[:, :N_EXPERTS].set(router_w)
    b = jnp.full((1, LANES), -1e30, F32).at[0, :N_EXPERTS].set(router_b)
    return pl.pallas_call(
        _router_kernel,
        grid=(T // tm,),
        in_specs=[pl.BlockSpec((tm, D), lambda i: (i, 0)),
                  pl.BlockSpec((D, LANES), lambda i: (0, 0)),
                  pl.BlockSpec((1, LANES), lambda i: (0, 0))],
        out_specs=[pl.BlockSpec((tm, LANES), lambda i: (i, 0))] * 2,
        out_shape=[jax.ShapeDtypeStruct((T, LANES), jnp.int32), jax.ShapeDtypeStruct((T, LANES), F32)],
        compiler_params=_cparams("parallel"),
        name="router_topk",
    )(h, w, b)


def _row_gather(src_ref, dst_ref, sem, index_of, n):
    def copy(r):
        return pltpu.make_async_copy(src_ref.at[pl.ds(index_of(r), 1), :], dst_ref.at[pl.ds(r, 1), :], sem)

    def start():
        for r in range(n):
            copy(r).start()

    def wait():
        for r in range(n):
            copy(r).wait()

    return start, wait


def _expert_kernel(be_ref, nu_ref, idx_ref, nidx_ref, h_ref, wgu_ref, bgu_ref, wd_ref, bd_ref, o_ref, xbuf, sem):
    i = pl.program_id(0)
    n_used = nu_ref[0]
    slot = lax.rem(i, 2)

    def gather(ids_ref, s):
        return _row_gather(h_ref, xbuf.at[s], sem.at[s], lambda r: ids_ref[0, 0, r], MOE_BLOCK)

    @pl.when(i == 0)
    def _():
        gather(idx_ref, 0)[0]()

    @pl.when(i + 1 < n_used)
    def _():
        gather(nidx_ref, 1 - slot)[0]()

    @pl.when(i < n_used)
    def _():
        gather(idx_ref, slot)[1]()
        gu = jnp.dot(xbuf[slot].astype(BF16), wgu_ref[0], preferred_element_type=F32) + bgu_ref[0]
        gate = jnp.minimum(gu[:, :D_EXPERT], SWIGLU_LIMIT)
        up = jnp.clip(gu[:, D_EXPERT:], -SWIGLU_LIMIT, SWIGLU_LIMIT)
        act = (up + 1.0) * gate * jax.nn.sigmoid(gate * SWIGLU_ALPHA)
        o_ref[...] = jnp.dot(act.astype(BF16), wd_ref[0], preferred_element_type=F32) + bd_ref[0]

    @pl.when(i >= n_used)
    def _():
        o_ref[...] = jnp.zeros_like(o_ref)


def _experts(h, row_tok, block_e, n_used, w_gu, b_gu, w_down, b_down):
    D = h.shape[1]
    n_rows = row_tok.shape[0]
    nb = n_rows // MOE_BLOCK
    ids = row_tok.reshape(nb, 1, MOE_BLOCK)
    grid_spec = pltpu.PrefetchScalarGridSpec(
        num_scalar_prefetch=2,
        grid=(nb,),
        in_specs=[pl.BlockSpec((1, 1, MOE_BLOCK), lambda i, be, nu: (i, 0, 0), memory_space=pltpu.SMEM),
                  pl.BlockSpec((1, 1, MOE_BLOCK), lambda i, be, nu: (jnp.minimum(i + 1, nb - 1), 0, 0),
                               memory_space=pltpu.SMEM),
                  pl.BlockSpec(memory_space=pl.ANY),
                  pl.BlockSpec((1, D, 2 * D_EXPERT), lambda i, be, nu: (be[i], 0, 0)),
                  pl.BlockSpec((1, 1, 2 * D_EXPERT), lambda i, be, nu: (be[i], 0, 0)),
                  pl.BlockSpec((1, D_EXPERT, D), lambda i, be, nu: (be[i], 0, 0)),
                  pl.BlockSpec((1, 1, D), lambda i, be, nu: (be[i], 0, 0))],
        out_specs=pl.BlockSpec((MOE_BLOCK, D), lambda i, be, nu: (i, 0)),
        scratch_shapes=[pltpu.VMEM((2, MOE_BLOCK, D), F32), pltpu.SemaphoreType.DMA((2,))],
    )
    return pl.pallas_call(
        _expert_kernel,
        grid_spec=grid_spec,
        out_shape=jax.ShapeDtypeStruct((n_rows, D), F32),
        compiler_params=_cparams("arbitrary"),
        name="experts",
    )(block_e, n_used, ids, ids, h, w_gu, b_gu.reshape(N_EXPERTS, 1, -1), w_down,
      b_down.reshape(N_EXPERTS, 1, -1))


def _combine_kernel(idx_ref, nidx_ref, g_ref, res_ref, mod_ref, y_ref, o_ref, buf, sem, *, ctx_rows, tm, nt):
    i = pl.program_id(0)
    slot = lax.rem(i, 2)

    def gathers(ids_ref, s):
        return [_row_gather(y_ref, buf.at[s, k], sem.at[s], functools.partial(lambda r, k: ids_ref[0, k, r], k=k), tm)
                for k in range(TOP_K)]

    @pl.when(i == 0)
    def _():
        for start, _ in gathers(idx_ref, 0):
            start()

    @pl.when(i + 1 < nt)
    def _():
        for start, _ in gathers(nidx_ref, 1 - slot):
            start()

    for _, wait in gathers(idx_ref, slot):
        wait()
    g = g_ref[...]
    y = g[:, 0:1] * buf[slot, 0]
    for k in range(1, TOP_K):
        y = y + g[:, k:k + 1] * buf[slot, k]
    rows = i * tm + lax.broadcasted_iota(jnp.int32, (tm, 1), 0)
    m = jnp.where(rows < ctx_rows, mod_ref[1:2, :], mod_ref[0:1, :])
    o_ref[...] = res_ref[...] + m * y


def _combine(yb, dest, gates, resid, mod, ctx_rows):
    T, D = resid.shape
    tm = _tile(T, (128,))
    nt = T // tm
    ids = jnp.swapaxes(dest.reshape(nt, tm, TOP_K), 1, 2)
    return pl.pallas_call(
        functools.partial(_combine_kernel, ctx_rows=ctx_rows, tm=tm, nt=nt),
        grid=(nt,),
        in_specs=[pl.BlockSpec((1, TOP_K, tm), lambda i: (i, 0, 0), memory_space=pltpu.SMEM),
                  pl.BlockSpec((1, TOP_K, tm), lambda i: (jnp.minimum(i + 1, nt - 1), 0, 0),
                               memory_space=pltpu.SMEM),
                  pl.BlockSpec((tm, LANES), lambda i: (i, 0)),
                  pl.BlockSpec((tm, D), lambda i: (i, 0)),
                  pl.BlockSpec((8, D), lambda i: (0, 0)),
                  pl.BlockSpec(memory_space=pl.ANY)],
        out_specs=pl.BlockSpec((tm, D), lambda i: (i, 0)),
        out_shape=jax.ShapeDtypeStruct((T, D), F32),
        scratch_shapes=[pltpu.VMEM((2, TOP_K, tm, D), F32), pltpu.SemaphoreType.DMA((2,))],
        compiler_params=_cparams("arbitrary"),
        name="moe_combine",
    )(ids, ids, gates, resid, mod, yb)


def _moe(h2, resid, mod, ctx_rows, router_w, router_b, w_gu, b_gu, w_down, b_down):
    T, D = h2.shape
    e128, g128 = _router(h2, router_w, router_b)
    top_e = e128[:, :TOP_K]
    onehot = (top_e[:, :, None] == jnp.arange(N_EXPERTS, dtype=jnp.int32)).astype(jnp.int32).sum(1)
    counts = onehot.sum(0)
    rank = jnp.cumsum(onehot, axis=0) - onehot
    pcounts = (counts + MOE_BLOCK - 1) // MOE_BLOCK * MOE_BLOCK
    pends = jnp.cumsum(pcounts)
    pstarts = pends - pcounts
    dest = (pstarts[top_e] + jnp.take_along_axis(rank, top_e, axis=1)).astype(jnp.int32)
    n_blocks = -(-T * TOP_K // MOE_BLOCK) + N_EXPERTS
    n_rows = n_blocks * MOE_BLOCK
    flat_dest = dest.reshape(-1)
    tok = jnp.repeat(jnp.arange(T, dtype=jnp.int32), TOP_K)
    row_tok = jnp.zeros((n_rows,), jnp.int32).at[flat_dest].set(tok)
    block_e = jnp.minimum(
        jnp.searchsorted(pends, jnp.arange(n_blocks, dtype=pends.dtype) * MOE_BLOCK, side='right'),
        N_EXPERTS - 1).astype(jnp.int32)
    n_used = (pends[-1] // MOE_BLOCK).astype(jnp.int32).reshape(1)
    yb = _experts(h2, row_tok, block_e, n_used, w_gu, b_gu, w_down, b_down)
    return _combine(yb, dest, g128, resid, mod, ctx_rows)


def _rope_tables(n_ctx, S):
    half = MLA_ROPE // 2
    inv = ROPE_THETA ** (-jnp.arange(0, half, 2, dtype=F32) / half)
    rows = jnp.repeat(jnp.arange(S // GRID_W), GRID_W).astype(F32)[:, None] * inv
    cols = jnp.tile(jnp.arange(GRID_W), S // GRID_W).astype(F32)[:, None] * inv
    ang = jnp.concatenate([rows, rows, cols, cols], -1)
    cos = jnp.concatenate([jnp.ones((n_ctx, MLA_ROPE), F32), jnp.cos(ang)], 0)
    sin = jnp.concatenate([jnp.zeros((n_ctx, MLA_ROPE), F32), jnp.sin(ang)], 0)
    first = (jnp.arange(LANES) % (MLA_ROPE // 2)) < MLA_ROPE // 4
    cos2, sin2 = (jnp.concatenate([t, t], -1) * (SWA_SCALE * LOG2E) for t in (cos, sin))
    swa_tabs = (cos2, jnp.where(first, -sin2, 0.0), jnp.where(first, 0.0, sin2))
    return cos, sin, swa_tabs


def _rope(x, cos, sin):
    q = MLA_ROPE // 4

    def rot(p):
        return jnp.concatenate([-p[..., q:], p[..., :q]], -1)

    xr = jnp.concatenate([rot(x[..., :2 * q]), rot(x[..., 2 * q:])], -1)
    return x * cos[:, None, :] + xr * sin[:, None, :]


def _pad_k(a, k):
    return jnp.pad(a, ((0, 0), (0, k - a.shape[1])))


DECAY_GAIN = float(np.exp(-0.5))
LORA_W = 1024
O_WD_B, O_AD_F, O_AD_B, O_GD = R_W, 2 * R_W, 2 * R_W + R_A, 2 * R_W + 2 * R_A


def _seg_matrix():
    i = lax.broadcasted_iota(jnp.int32, (LANES, LANES), 0) // N_B
    j = lax.broadcasted_iota(jnp.int32, (LANES, LANES), 1) // N_B
    return (i == j).astype(BF16)


def _head_sums(x, seg):
    outs = []
    for c in range(x.shape[1] // LANES):
        rem = x[:, c * LANES:(c + 1) * LANES]
        tot = None
        for _ in range(3):
            hi = rem.astype(BF16)
            y = jnp.dot(hi, seg, preferred_element_type=F32)
            tot = y if tot is None else tot + y
            rem = rem - hi.astype(F32)
        outs.append(tot)
    return jnp.concatenate(outs, axis=1)


def _rwkv_prep_kernel(pr_ref, pk_ref, pv_ref, pl_ref, hp_r, hp_k, hp_v, hp_l, hn_r, hn_k, hn_v, hn_l,
                      mu_r, mu_k, mu_v, mu_l, kk_g, ka_g, w0_ref, a0_ref, wwf, wwb, waf, wab, wg,
                      r_o, v_o, kk_o, wf_o, kdf_o, bf_o, wb_o, kdb_o, bb_o, g_o, *, tm, n_ctx, n_tok):
    row = pl.program_id(0) * tm + lax.broadcasted_iota(jnp.int32, (tm, 1), 0)
    local = lax.broadcasted_iota(jnp.int32, (tm, 1), 0)
    has_prev = (row != 0) & (row != n_ctx)
    has_next = (row != n_ctx - 1) & (row != n_tok - 1)

    def shifted(x_ref, hp_ref, hn_ref, mu_ref):
        x = x_ref[...]
        prev = jnp.where(local == 0, hp_ref[7:8, :], pltpu.roll(x, 1, axis=0))
        nxt = jnp.where(local == tm - 1, hn_ref[0:1, :], pltpu.roll(x, tm - 1, axis=0))
        prev = jnp.where(has_prev, prev, 0.0)
        nxt = jnp.where(has_next, nxt, 0.0)
        return x + mu_ref[0:1, :] * (prev - x) + mu_ref[1:2, :] * (nxt - x)

    r = shifted(pr_ref, hp_r, hn_r, mu_r)
    k = shifted(pk_ref, hp_k, hn_k, mu_k)
    v = shifted(pv_ref, hp_v, hn_v, mu_v)
    lx = shifted(pl_ref, hp_l, hn_l, mu_l)
    seg = _seg_matrix()
    kk = k * kk_g[...]
    kk = kk * lax.rsqrt(_head_sums(kk * kk, seg) + 1e-12)
    r_o[...] = r
    v_o[...] = v
    kk_o[...] = kk
    th = jnp.tanh(lx[:, 0:2 * LANES]).astype(BF16)
    lin = lx[:, LANES:3 * LANES].astype(BF16)
    w_pre = (jnp.dot(th[:, 0:LANES], wwf[...], preferred_element_type=F32),
             jnp.dot(th, wwb[...], preferred_element_type=F32))
    a_pre = (jnp.dot(lin, waf[...], preferred_element_type=F32),
             jnp.dot(lin[:, LANES:], wab[...], preferred_element_type=F32))
    for d, (w_o, kd_o, b_o) in enumerate(((wf_o, kdf_o, bf_o), (wb_o, kdb_o, bb_o))):
        wp = w0_ref[d:d + 1, :] + w_pre[d]
        w_o[...] = jnp.exp(-DECAY_GAIN * jax.nn.sigmoid(wp))
        a = jax.nn.sigmoid(a0_ref[d:d + 1, :] + a_pre[d])
        kd_o[...] = k * (1.0 + (a - 1.0) * ka_g[...])
        b_o[...] = kk * a
    gd = jax.nn.sigmoid(lx[:, O_GD:O_GD + R_G]).astype(BF16)
    g_o[...] = jnp.dot(gd, wg[...], preferred_element_type=F32)


def _rows8(a):
    return jnp.pad(a, ((0, 8 - a.shape[0]), (0, 0))).astype(F32)


def _rwkv_prep(p, lo, lp, n_ctx):
    T = p.shape[0]
    tm = 128
    assert T % tm == 0 and n_ctx % tm == 0
    hb = tm // 8
    last8 = T // 8 - 1
    main = lambda w, off: pl.BlockSpec((tm, w), lambda i: (i, off // w))
    halo_p = lambda w, off: pl.BlockSpec((8, w), lambda i: (jnp.maximum(i * hb - 1, 0), off // w))
    halo_n = lambda w, off: pl.BlockSpec((8, w), lambda i: (jnp.minimum((i + 1) * hb, last8), off // w))
    groups = ((W_B, lo['r']), (W_B, lo['k']), (W_B, lo['v']), (LORA_W, lo['lora']))
    for w, off in groups:
        assert off % w == 0
    full = lambda a: pl.BlockSpec(a.shape, lambda i: (0, 0))
    mu = lp['rwkv_mu']
    mus = [_rows8(mu[:, 0:W_B]), _rows8(mu[:, W_B:2 * W_B]), _rows8(mu[:, 2 * W_B:3 * W_B]),
           _rows8(jnp.pad(mu[:, 3 * W_B:], ((0, 0), (0, LORA_W - (C_RWKV - 3 * W_B)))))]

    def slot(w, lo_row, n_rows):
        return jnp.pad(w, ((lo_row, n_rows - lo_row - w.shape[0]), (0, 0))).astype(BF16)

    params = mus + [lp['rwkv_k_k'].reshape(1, W_B), lp['rwkv_k_a'].reshape(1, W_B),
                    _rows8(lp['rwkv_w0']), _rows8(lp['rwkv_a0']),
                    slot(lp['rwkv_w_up'][0], 0, LANES), slot(lp['rwkv_w_up'][1], O_WD_B, 2 * LANES),
                    slot(lp['rwkv_a_up'][0], O_AD_F - LANES, 2 * LANES),
                    slot(lp['rwkv_a_up'][1], O_AD_B - 2 * LANES, LANES),
                    lp['rwkv_g_up'].astype(BF16)]
    in_specs = ([main(w, off) for w, off in groups] + [halo_p(w, off) for w, off in groups]
                + [halo_n(w, off) for w, off in groups] + [full(a) for a in params])
    outs = pl.pallas_call(
        functools.partial(_rwkv_prep_kernel, tm=tm, n_ctx=n_ctx, n_tok=T),
        grid=(T // tm,),
        in_specs=in_specs,
        out_specs=[pl.BlockSpec((tm, W_B), lambda i: (i, 0))] * 10,
        out_shape=[jax.ShapeDtypeStruct((T, W_B), F32)] * 10,
        compiler_params=_cparams("parallel"),
        name="rwkv_prep",
    )(*([p] * 12 + params))
    return outs


def _rwkv_out_kernel(of_ref, ob_ref, r_ref, v_ref, kdf_ref, kdb_ref, g_ref, lng_ref, lnb_ref, rk_ref, o_ref):
    seg = _seg_matrix()
    o = of_ref[...] + ob_ref[...]
    d = o - _head_sums(o, seg) * (1.0 / N_B)
    var = _head_sums(d * d, seg) * (1.0 / N_B)
    on = d * lax.rsqrt(var + GN_EPS) * lng_ref[...] + lnb_ref[...]
    coef = _head_sums(r_ref[...] * (kdf_ref[...] + kdb_ref[...]) * rk_ref[...], seg)
    o_ref[...] = ((on + coef * v_ref[...]) * g_ref[...]).astype(o_ref.dtype)


def _rwkv_out(o_f, o_b, r, v, kd_f, kd_b, g, lp):
    T = r.shape[0]
    tm = _tile(T, (128,))
    blk = pl.BlockSpec((tm, W_B), lambda i: (i, 0))
    par = pl.BlockSpec((1, W_B), lambda i: (0, 0))
    return pl.pallas_call(
        _rwkv_out_kernel,
        grid=(T // tm,),
        in_specs=[blk] * 7 + [par] * 3,
        out_specs=blk,
        out_shape=jax.ShapeDtypeStruct((T, W_B), BF16),
        compiler_params=_cparams("parallel"),
        name="rwkv_out",
    )(o_f, o_b, r, v, kd_f, kd_b, g, lp['rwkv_ln_g'].reshape(1, W_B), lp['rwkv_ln_b'].reshape(1, W_B),
      lp['rwkv_r_k'].reshape(1, W_B))


def _rwkv_mixer(p, lo, lp, n_ctx):
    r, v, kk, w_f, kd_f, b_f, w_b, kd_b, b_b, g = _rwkv_prep(p, lo, lp, n_ctx)
    o_f, o_b = _rwkv_scan(r, v, kk, (w_f, w_b), (kd_f, kd_b), (b_f, b_b), n_ctx=n_ctx)
    return _rwkv_out(o_f, o_b, r, v, kd_f, kd_b, g, lp)


def _in_layout(D):
    o = {}
    o['g'] = 0
    o['r'] = N_BRANCH * D
    o['k'] = o['r'] + W_B
    o['v'] = o['k'] + W_B
    o['sq'] = o['v'] + W_B
    o['q'] = o['sq'] + W_C
    o['lora'] = o['q'] + Q_LORA
    o['kv'] = o['lora'] + LORA_W
    o['sk'] = o['kv'] + KV_LORA
    o['sv'] = o['sk'] + KV_C * HD_C
    o['kr'] = o['sv'] + KV_C * HD_C
    o['end'] = -(-(o['kr'] + MLA_ROPE) // 1024) * 1024
    return o


def _prep_w_in(w_in, D):
    s = [int(v) for v in np.cumsum([0, Q_LORA, KV_LORA, MLA_ROPE, C_RWKV, W_C, KV_C * HD_C, KV_C * HD_C,
                                    N_BRANCH * D])]
    seg = lambda i: w_in[:, s[i]:s[i + 1]]
    rw = seg(3)
    zeros = lambda n: jnp.zeros((w_in.shape[0], n), w_in.dtype)
    lo = _in_layout(D)
    cols = [seg(7), rw[:, :3 * W_B], seg(4), seg(0), rw[:, 3 * W_B:], zeros(LORA_W - (C_RWKV - 3 * W_B)), seg(1),
            seg(5), seg(6), seg(2), zeros(lo['end'] - lo['kr'] - MLA_ROPE)]
    return jnp.concatenate(cols, -1).astype(BF16)


def _rot_cols(w):
    q = MLA_ROPE // 4
    return jnp.concatenate([-w[..., q:2 * q], w[..., :q], -w[..., 3 * q:], w[..., 2 * q:3 * q]], -1)


def _prep_wq_up(wq_up):
    w = wq_up.reshape(Q_LORA, H_A, MLA_NOPE + MLA_ROPE)
    w = jnp.concatenate([w, _rot_cols(w[..., MLA_NOPE:])], -1)
    return w.reshape(Q_LORA, H_A * MLA_QW).astype(BF16)


def _layer(xall, mod, lp, rope, n_ctx):
    T, D = xall.shape
    S = T - n_ctx
    cos, sin, swa_tabs = rope
    md = [mod[:, j * D:(j + 1) * D] for j in range(6)]
    lo = _in_layout(D)

    h1 = _rownorm(xall, lp['norm1_g'], shift=md[0], scale=md[1], ctx_rows=n_ctx, name="norm1")
    w_in = _prep_w_in(lp['w_in'], D)
    p = _matmul(h1, w_in, tm=_tile(T, (1280, 640, 256, 128)), tn=512, name="w_in")

    qn = _rownorm(p, lp['mla_qn_g'], col_off=lo['q'], width=Q_LORA, name="mla_qnorm")
    q_tab = jnp.concatenate([jnp.ones((T, MLA_NOPE), F32), cos, sin], -1) * (MLA_SCALE * LOG2E)
    q = _matmul(qn, _prep_wq_up(lp['mla_wq_up']), row_table=q_tab, out_dtype=BF16, name="mla_q_up")
    kvn = _rownorm(p, lp['mla_kvn_g'], col_off=lo['kv'], width=KV_LORA, name="mla_kvnorm")
    kv = _matmul(kvn, lp['mla_wkv_up'].astype(BF16), out_dtype=BF16, name="mla_kv_up")
    kr = _rope(p[:, None, lo['kr']:lo['kr'] + MLA_ROPE], cos, sin)[:, 0]
    kr = jnp.concatenate([kr, kr], -1).astype(BF16)
    oa_x = _mla_attention(q[n_ctx:], kv, kr, n_q=S, n_keys=T, name="mla_latent")
    oa_c = _mla_attention(q[:n_ctx], kv, kr, n_q=n_ctx, n_keys=n_ctx, name="mla_context")
    o_a = jnp.concatenate([oa_c, oa_x], 0)

    o_b = _rwkv_mixer(p, lo, lp, n_ctx)

    kvw = KV_C * HD_C
    kc = _rope(p[:, lo['sk']:lo['sk'] + kvw].reshape(T, KV_C, HD_C), cos, sin).reshape(T, kvw)
    vc = p[:, lo['sv']:lo['sv'] + kvw]
    sink_rows = jnp.repeat(lp['swa_sink'].astype(F32).reshape(KV_C, G_C) * LOG2E, 128, axis=1)
    sink_rows = jnp.broadcast_to(sink_rows.reshape(KV_C, G_C * 128, 1), (KV_C, G_C * 128, LANES))
    oc_x = _swa_attention(p, lo['sq'], swa_tabs, kc, vc, sink_rows, n_ctx=n_ctx, band=True, name="swa_latent")
    oc_c = _swa_attention(p, lo['sq'], swa_tabs, kc, vc, sink_rows, n_ctx=n_ctx, band=False, name="swa_context")
    o_c = jnp.concatenate([oc_c, oc_x], 0)

    m = _merge(jnp.stack([o_a, o_b, o_c], 0), lp['w_branch'].astype(BF16), p, lo['g'])
    x1 = _matmul(m, lp['w_out'].astype(BF16), resid=xall, mod=md[2], ctx_rows=n_ctx, name="w_out")
    h2 = _rownorm(x1, lp['norm2_g'], shift=md[3], scale=md[4], ctx_rows=n_ctx, out_dtype=F32, name="norm2")
    return _moe(h2, x1, md[5], n_ctx, lp['router_w'], lp['router_b'], lp['exp_w_gu'].astype(BF16),
                lp['exp_b_gu'], lp['exp_w_down'].astype(BF16), lp['exp_b_down'])


def kernel(x, c, ctx, c_ctx, ada_w, ada_b, norm1_g, w_in, mla_qn_g, mla_kvn_g, mla_wq_up, mla_wkv_up, rwkv_mu, rwkv_w0, rwkv_w_up, rwkv_a0, rwkv_a_up, rwkv_g_up, rwkv_k_k, rwkv_k_a, rwkv_r_k, rwkv_ln_g, rwkv_ln_b, swa_sink, w_branch, w_out, norm2_g, router_w, router_b, exp_w_gu, exp_b_gu, exp_w_down, exp_b_down, final_g):
    B, S, D = x.shape
    n_ctx = ctx.shape[1]
    assert B == 1 and c.shape[0] == 1
    depth = ada_w.shape[0]
    rope = _rope_tables(n_ctx, S)
    xall = jnp.concatenate([ctx[0], x[0]], 0)
    cond = jax.nn.silu(jnp.concatenate([c, c_ctx[None], jnp.zeros((6, D), F32)], 0))
    for i in range(depth):
        lp = dict(norm1_g=norm1_g[i], w_in=w_in[i],
                  mla_qn_g=mla_qn_g[i], mla_kvn_g=mla_kvn_g[i], mla_wq_up=mla_wq_up[i], mla_wkv_up=mla_wkv_up[i],
                  rwkv_mu=rwkv_mu[i], rwkv_w0=rwkv_w0[i], rwkv_w_up=rwkv_w_up[i], rwkv_a0=rwkv_a0[i],
                  rwkv_a_up=rwkv_a_up[i], rwkv_g_up=rwkv_g_up[i], rwkv_k_k=rwkv_k_k[i], rwkv_k_a=rwkv_k_a[i],
                  rwkv_r_k=rwkv_r_k[i], rwkv_ln_g=rwkv_ln_g[i], rwkv_ln_b=rwkv_ln_b[i], swa_sink=swa_sink[i],
                  w_branch=w_branch[i], w_out=w_out[i], norm2_g=norm2_g[i], router_w=router_w[i],
                  router_b=router_b[i], exp_w_gu=exp_w_gu[i], exp_b_gu=exp_b_gu[i],
                  exp_w_down=exp_w_down[i], exp_b_down=exp_b_down[i])
        mod = _matmul(cond, ada_w, b_slab=i, bias=ada_b[i], tm=8, tn=1024, name="adaln")
        xall = _layer(xall, mod, lp, rope, n_ctx)
    out = _rownorm(xall, final_g, row_off=n_ctx, rows=S, out_dtype=F32, name="final_norm")
    return out.reshape(B, S, D)
```

```python
import functools

import numpy as np
import jax
import jax.numpy as jnp
from jax import lax
from jax.experimental import pallas as pl
from jax.experimental.pallas import tpu as pltpu

F32 = jnp.float32
BF16 = jnp.bfloat16

GRID_W = 64
ROPE_THETA = 10000.0
NORM_EPS = 1e-6
H_A, MLA_NOPE, MLA_ROPE, MLA_V = 16, 128, 64, 128
Q_LORA, KV_LORA = 1024, 512
MLA_SCALE = (MLA_NOPE + MLA_ROPE) ** -0.5
LOG2E = 1.4426950408889634
MLA_QW = 256
H_B, N_B = 32, 64
W_B = H_B * N_B
R_W, R_A, R_G = 96, 96, 256
GN_EPS = 64e-5
H_C, KV_C, HD_C = 32, 4, 64
G_C = H_C // KV_C
WINDOW = 128
W_C = H_C * HD_C
SWA_SCALE = HD_C ** -0.5
N_BRANCH = 3
N_EXPERTS, TOP_K, D_EXPERT = 32, 4, 512
SWIGLU_LIMIT, SWIGLU_ALPHA = 7.0, 1.702
MOE_BLOCK = 128
C_RWKV = 3 * W_B + 2 * R_W + 2 * R_A + R_G

LANES = 128
VMEM_LIMIT = 56 * 1024 * 1024


def _cparams(*sem):
    return pltpu.CompilerParams(dimension_semantics=sem, vmem_limit_bytes=VMEM_LIMIT)


def _lane_tile(x, k):
    return x if k == 1 else jnp.concatenate([x] * k, axis=1)


def _tile(n, prefs):
    for t in prefs:
        if n % t == 0:
            return t
    return n


def _mm_kernel(*refs, nk, has_bias, resid, has_tab, ctx_rows, tm, precision):
    it = iter(refs)
    a_ref, b_ref = next(it), next(it)
    bias_ref = next(it) if has_bias else None
    res_ref, mod_ref = (next(it), next(it)) if resid else (None, None)
    tab_ref = next(it) if has_tab else None
    o_ref = next(it)
    acc_ref = next(it) if nk > 1 else None

    if precision is None:
        part = jnp.dot(a_ref[...].astype(BF16), b_ref[...].astype(BF16), preferred_element_type=F32)
    else:
        part = jnp.dot(a_ref[...], b_ref[...], preferred_element_type=F32, precision=precision)

    def finish(acc):
        if has_bias:
            acc = acc + bias_ref[...]
        if resid:
            rows = pl.program_id(0) * tm + lax.broadcasted_iota(jnp.int32, (tm, 1), 0)
            m = jnp.where(rows < ctx_rows, mod_ref[1:2, :], mod_ref[0:1, :])
            acc = res_ref[...] + m * acc
        if has_tab:
            acc = acc * _lane_tile(tab_ref[...], acc.shape[1] // tab_ref.shape[1])
        o_ref[...] = acc.astype(o_ref.dtype)

    if nk == 1:
        finish(part)
    else:
        k = pl.program_id(2)

        @pl.when(k == 0)
        def _():
            acc_ref[...] = part

        @pl.when(k > 0)
        def _():
            acc_ref[...] += part

        @pl.when(k == nk - 1)
        def _():
            finish(acc_ref[...])


def _matmul(a, b, *, a_col_off=0, k_dim=None, bias=None, resid=None, mod=None, ctx_rows=0, row_table=None,
            b_slab=0, out_dtype=F32, tm=None, tn=None, tk=None, precision=None, name="matmul"):
    M = a.shape[0]
    K, N = b.shape[-2:]
    if k_dim is None:
        assert a.shape[1] == K and a_col_off == 0
    tm = tm or _tile(M, (640, 512, 256, 128))
    tn = tn or _tile(N, (1024, 896, 512, 256, 128))
    tk = tk or K
    assert M % tm == 0 and N % tn == 0 and K % tk == 0 and a_col_off % tk == 0
    nk = K // tk
    koff = a_col_off // tk
    if b.ndim == 3:
        b_spec = pl.BlockSpec((None, tk, tn), lambda i, j, k: (b_slab, k, j))
    else:
        b_spec = pl.BlockSpec((tk, tn), lambda i, j, k: (k, j))
    in_specs = [pl.BlockSpec((tm, tk), lambda i, j, k: (i, k + koff)), b_spec]
    args = [a, b]
    if bias is not None:
        in_specs.append(pl.BlockSpec((1, tn), lambda i, j, k: (0, j)))
        args.append(bias.reshape(1, N).astype(F32))
    if resid is not None:
        in_specs.append(pl.BlockSpec((tm, tn), lambda i, j, k: (i, j)))
        in_specs.append(pl.BlockSpec((8, tn), lambda i, j, k: (0, j)))
        args += [resid, mod]
    if row_table is not None:
        assert tn % row_table.shape[1] == 0
        in_specs.append(pl.BlockSpec((tm, row_table.shape[1]), lambda i, j, k: (i, 0)))
        args.append(row_table)
    kern = functools.partial(_mm_kernel, nk=nk, has_bias=bias is not None, resid=resid is not None,
                             has_tab=row_table is not None, ctx_rows=ctx_rows, tm=tm, precision=precision)
    return pl.pallas_call(
        kern,
        grid=(M // tm, N // tn, nk),
        in_specs=in_specs,
        out_specs=pl.BlockSpec((tm, tn), lambda i, j, k: (i, j)),
        out_shape=jax.ShapeDtypeStruct((M, N), out_dtype),
        scratch_shapes=[pltpu.VMEM((tm, tn), F32)] if nk > 1 else [],
        compiler_params=_cparams("parallel", "parallel", "arbitrary"),
        name=name,
    )(*args)


def _norm_kernel(*refs, eps, modulated, ctx_rows, tm):
    if modulated:
        x_ref, g_ref, sh_ref, sc_ref, o_ref = refs
    else:
        x_ref, g_ref, o_ref = refs
    x = x_ref[...].astype(F32)
    y = x * lax.rsqrt(jnp.mean(x * x, axis=-1, keepdims=True) + eps) * g_ref[...]
    if modulated:
        rows = pl.program_id(0) * tm + lax.broadcasted_iota(jnp.int32, (tm, 1), 0)
        isctx = rows < ctx_rows
        sc = jnp.where(isctx, sc_ref[1:2, :], sc_ref[0:1, :])
        sh = jnp.where(isctx, sh_ref[1:2, :], sh_ref[0:1, :])
        y = y * (1.0 + sc) + sh
    o_ref[...] = y.astype(o_ref.dtype)


def _rownorm(x, g, *, col_off=0, width=None, row_off=0, rows=None, shift=None, scale=None,
             ctx_rows=0, out_dtype=BF16, eps=NORM_EPS, name="rownorm"):
    width = width or x.shape[1]
    rows = rows or x.shape[0]
    tm = _tile(rows, (256, 128))
    assert col_off % width == 0 and row_off % tm == 0 and rows % tm == 0
    cb, rb = col_off // width, row_off // tm
    modulated = shift is not None
    in_specs = [pl.BlockSpec((tm, width), lambda i: (i + rb, cb)),
                pl.BlockSpec((1, width), lambda i: (0, 0))]
    args = [x, g.reshape(1, width).astype(F32)]
    if modulated:
        in_specs += [pl.BlockSpec((8, width), lambda i: (0, 0))] * 2
        args += [shift, scale]
    kern = functools.partial(_norm_kernel, eps=eps, modulated=modulated, ctx_rows=ctx_rows, tm=tm)
    return pl.pallas_call(
        kern,
        grid=(rows // tm,),
        in_specs=in_specs,
        out_specs=pl.BlockSpec((tm, width), lambda i: (i, 0)),
        out_shape=jax.ShapeDtypeStruct((rows, width), out_dtype),
        compiler_params=_cparams("parallel"),
        name=name,
    )(*args)


def _flash_kernel(q_ref, kn_ref, kr_ref, v_ref, o_ref, m_ref, acc_ref, *, nk, sub):
    kj = pl.program_id(2)

    @pl.when(kj == 0)
    def _():
        m_ref[...] = jnp.full_like(m_ref, -jnp.inf)
        acc_ref[...] = jnp.zeros_like(acc_ref)

    q = q_ref[...]
    m, acc = m_ref[...], acc_ref[...]
    ones = jnp.ones((sub, MLA_V), BF16)
    for c in range(kn_ref.shape[0] // sub):
        rows = slice(c * sub, (c + 1) * sub)
        k = jnp.concatenate([kn_ref[rows, :], kr_ref[rows, :]], axis=-1)
        s = lax.dot_general(q, k, (((1,), (1,)), ((), ())), preferred_element_type=F32)
        m_new = jnp.maximum(m, jnp.max(s, axis=-1, keepdims=True))
        p = jnp.exp2(s - _lane_tile(m_new, sub // LANES)).astype(BF16)
        v1 = jnp.concatenate([v_ref[rows, :], ones], axis=-1)
        acc = _lane_tile(jnp.exp2(m - m_new), 2) * acc + jnp.dot(p, v1, preferred_element_type=F32)
        m = m_new
    m_ref[...] = m
    acc_ref[...] = acc

    @pl.when(kj == nk - 1)
    def _():
        acc = acc_ref[...]
        o_ref[...] = (acc[:, :MLA_V] / acc[:, MLA_V:]).astype(o_ref.dtype)


def _mla_attention(q, kv, kr, *, n_q, n_keys, name):
    tq = _tile(n_q, (2048, 1024, 512, 256, 128))
    tk = _tile(n_keys, (3328, 1280, 640, 512, 256, 128))
    nk = n_keys // tk
    sub = _tile(tk, (256, 128))
    return pl.pallas_call(
        functools.partial(_flash_kernel, nk=nk, sub=sub),
        grid=(H_A, n_q // tq, nk),
        in_specs=[pl.BlockSpec((tq, MLA_QW), lambda h, i, j: (i, h)),
                  pl.BlockSpec((tk, MLA_NOPE), lambda h, i, j: (j, 2 * h)),
                  pl.BlockSpec((tk, LANES), lambda h, i, j: (j, 0)),
                  pl.BlockSpec((tk, MLA_V), lambda h, i, j: (j, 2 * h + 1))],
        out_specs=pl.BlockSpec((tq, MLA_V), lambda h, i, j: (i, h)),
        out_shape=jax.ShapeDtypeStruct((n_q, H_A * MLA_V), BF16),
        scratch_shapes=[pltpu.VMEM((tq, LANES), F32), pltpu.VMEM((tq, 2 * MLA_V), F32)],
        compiler_params=_cparams("parallel", "parallel", "arbitrary"),
        name=name,
    )(q, kv, kr, kv)


def _swa_kernel(*refs, band, nb):
    if band:
        (q_ref, c_ref, s1_ref, s2_ref, sink_ref, kc_ref, vc_ref, kp_ref, k0_ref, kn_ref, vp_ref, v0_ref, vn_ref,
         o_ref) = refs
    else:
        q_ref, c_ref, s1_ref, s2_ref, sink_ref, kc_ref, vc_ref, o_ref = refs
    i = pl.program_id(0)
    qb = q_ref.shape[0]
    cos, s1, s2 = c_ref[...], s1_ref[...], s2_ref[...]

    def roped(x):
        return (x * cos + pltpu.roll(x, LANES - HD_C // 4, axis=1) * s1 + pltpu.roll(x, HD_C // 4, axis=1) * s2)

    lane = lax.broadcasted_iota(jnp.int32, (1, LANES), 1)
    low = lane < HD_C
    if band:
        r = lax.broadcasted_iota(jnp.int32, (qb, qb), 0)
        c = lax.broadcasted_iota(jnp.int32, (qb, qb), 1)
        ok_prev = c - r + jnp.where(i > 0, 0, -4 * qb) >= qb - WINDOW
        ok_next = r - c + jnp.where(i < nb - 1, 0, -4 * qb) >= qb - WINDOW

    def masked(s, ok):
        return jnp.where(ok[None], s.reshape(G_C, qb, qb), -jnp.inf).reshape(G_C * qb, qb)

    for g in range(KV_C):
        grp, upper = g // 2, g % 2 == 1
        mine = ~low if upper else low

        def both_halves(x_ref):
            x = x_ref[:, grp * LANES:(grp + 1) * LANES]
            return jnp.where(mine, x, pltpu.roll(x, HD_C, axis=1))

        qs = []
        for cgrp in range(G_C // 2):
            col = (g * (G_C // 2) + cgrp) * LANES
            x = roped(q_ref[:, col:col + LANES])
            qs += [jnp.where(low, x, 0.0), jnp.where(low, 0.0, x)]
        qs = jnp.concatenate(qs, axis=0).astype(BF16)

        def scores(k_ref):
            return lax.dot_general(qs, both_halves(k_ref).astype(BF16), (((1,), (1,)), ((), ())),
                                   preferred_element_type=F32)

        parts = [(scores(kc_ref), vc_ref)]
        if band:
            parts += [(masked(scores(kp_ref), ok_prev), vp_ref), (scores(k0_ref), v0_ref),
                      (masked(scores(kn_ref), ok_next), vn_ref)]
        sink = sink_ref[g]
        m = sink
        for s, _ in parts:
            m = jnp.maximum(m, jnp.max(s, axis=-1, keepdims=True))
        l_sink = jnp.exp2(sink - m)
        acc = [jnp.zeros((qb, LANES), F32)] * G_C
        for s, v_ref in parts:
            p = jnp.exp2(s - _lane_tile(m, s.shape[1] // LANES)).astype(BF16)
            v2 = both_halves(v_ref)
            v_half = (jnp.where(low, v2, 1.0).astype(BF16), jnp.where(low, 1.0, v2).astype(BF16))
            acc = [acc[j] + jnp.dot(p[j * qb:(j + 1) * qb], v_half[j % 2], preferred_element_type=F32)
                   for j in range(G_C)]
        for cgrp in range(G_C // 2):
            a_lo, a_hi = acc[2 * cgrp], acc[2 * cgrp + 1]
            lo_rows = slice((2 * cgrp) * qb, (2 * cgrp + 1) * qb)
            hi_rows = slice((2 * cgrp + 1) * qb, (2 * cgrp + 2) * qb)
            den = jnp.where(low, pltpu.roll(a_lo, HD_C, axis=1) + l_sink[lo_rows],
                            pltpu.roll(a_hi, HD_C, axis=1) + l_sink[hi_rows])
            tile = jnp.where(low, a_lo, a_hi) / den
            col = (g * (G_C // 2) + cgrp) * LANES
            o_ref[:, col:col + LANES] = tile.astype(o_ref.dtype)


def _swa_attention(p, q_off, tabs, kr, vv, sink_rows, *, n_ctx, band, name):
    T = p.shape[0]
    qb = 128
    assert WINDOW == qb and n_ctx % qb == 0 and q_off % W_C == 0
    cb = n_ctx // qb
    kvw = KV_C * HD_C
    nb, row0 = ((T - n_ctx) // qb, cb) if band else (cb, 0)
    tab = pl.BlockSpec((qb, LANES), lambda i: (i + row0, 0))
    in_specs = [pl.BlockSpec((qb, W_C), lambda i: (i + row0, q_off // W_C)), tab, tab, tab,
                pl.BlockSpec((KV_C, G_C * qb, LANES), lambda i: (0, 0, 0)),
                pl.BlockSpec((n_ctx, kvw), lambda i: (0, 0)),
                pl.BlockSpec((n_ctx, kvw), lambda i: (0, 0))]
    args = [p, *tabs, sink_rows, kr, vv]
    if band:
        blk = lambda off: pl.BlockSpec((qb, kvw), lambda i: (cb + jnp.clip(i + off, 0, nb - 1), 0))
        in_specs += [blk(-1), blk(0), blk(1)] * 2
        args += [kr, kr, kr, vv, vv, vv]
    return pl.pallas_call(
        functools.partial(_swa_kernel, band=band, nb=nb),
        grid=(nb,),
        in_specs=in_specs,
        out_specs=pl.BlockSpec((qb, W_C), lambda i: (i, 0)),
        out_shape=jax.ShapeDtypeStruct((nb * qb, W_C), BF16),
        compiler_params=_cparams("parallel"),
        name=name,
    )(*args)


N_PAIR = H_B // 2


def _scan_kernel(rf_ref, vf_ref, kkf_ref, wf_ref, kdf_ref, bf_ref, rb_ref, vb_ref, kkb_ref, wb_ref, kdb_ref,
                 bb_ref, of_ref, ob_ref, sf_ref, sb_ref, *, tb):
    @pl.when(pl.program_id(0) == 0)
    def _():
        sf_ref[...] = jnp.zeros_like(sf_ref)
        sb_ref[...] = jnp.zeros_like(sb_ref)

    lane_r = lax.broadcasted_iota(jnp.int32, (LANES, LANES), 0)
    lane_c = lax.broadcasted_iota(jnp.int32, (LANES, LANES), 1)
    seg = (lane_r // N_B == lane_c // N_B).astype(BF16)
    on_diag = (lax.broadcasted_iota(jnp.int32, (N_B, LANES), 0)
               == lax.broadcasted_iota(jnp.int32, (N_B, LANES), 1) % N_B)
    diag = on_diag.astype(F32)
    diag16 = on_diag.astype(BF16)

    def segsum(x):
        return jnp.dot(x, seg, preferred_element_type=F32)

    def tile_of(x, p):
        return x[p * N_B:(p + 1) * N_B, :]

    def one_step(i, tiles, s_ref):
        r_t, v_t, kk_t, w_t, kd_t, b_t = ([x[i:i + 1, :] for x in op] for op in tiles)
        sa = segsum(jnp.concatenate([(s_ref[p] * kk_t[p]).astype(BF16) for p in range(N_PAIR)], axis=0))
        vcol = segsum(jnp.concatenate(
            [diag16 * jnp.broadcast_to(v_t[p], (N_B, LANES)).astype(BF16) for p in range(N_PAIR)], axis=0))
        new = []
        for p in range(N_PAIR):
            s = s_ref[p] * w_t[p] - tile_of(sa, p) * b_t[p] + tile_of(vcol, p) * kd_t[p]
            s_ref[p] = s
            new.append((s * r_t[p]).astype(BF16))
        oc = segsum(jnp.concatenate(new, axis=0))
        return [jnp.sum(diag * tile_of(oc, p), axis=0, keepdims=True) for p in range(N_PAIR)]

    def load8(refs, row0):
        return [[ref[pl.ds(row0, 8), p * LANES:(p + 1) * LANES] for p in range(N_PAIR)] for ref in refs]

    def store8(o_ref, row0, rows):
        for p in range(N_PAIR):
            o_ref[pl.ds(row0, 8), p * LANES:(p + 1) * LANES] = jnp.concatenate([rows[i][p] for i in range(8)], 0)

    def group(g, carry):
        f0 = pl.multiple_of(g * 8, 8)
        b0 = pl.multiple_of(tb - 8 - g * 8, 8)
        f_tiles = load8((rf_ref, vf_ref, kkf_ref, wf_ref, kdf_ref, bf_ref), f0)
        b_tiles = load8((rb_ref, vb_ref, kkb_ref, wb_ref, kdb_ref, bb_ref), b0)
        f_rows, b_rows = [None] * 8, [None] * 8
        for i in range(8):
            f_rows[i] = one_step(i, f_tiles, sf_ref)
            b_rows[7 - i] = one_step(7 - i, b_tiles, sb_ref)
        store8(of_ref, f0, f_rows)
        store8(ob_ref, b0, b_rows)
        return carry

    lax.fori_loop(0, tb // 8, group, 0)


def _rwkv_scan(r, v, kk, ws, kds, bs, *, n_ctx):
    T = r.shape[0]
    tb = 128
    nb, cb = T // tb, n_ctx // tb
    assert T % tb == 0 and n_ctx % tb == 0

    def back(j):
        return jnp.where(j < cb, cb - 1 - j, nb - 1 - (j - cb))

    fwd = pl.BlockSpec((tb, W_B), lambda j: (j, 0))
    bwd = pl.BlockSpec((tb, W_B), lambda j: (back(j), 0))
    return pl.pallas_call(
        functools.partial(_scan_kernel, tb=tb),
        grid=(nb,),
        in_specs=[fwd] * 6 + [bwd] * 6,
        out_specs=[fwd, bwd],
        out_shape=[jax.ShapeDtypeStruct((T, W_B), F32)] * 2,
        scratch_shapes=[pltpu.VMEM((N_PAIR, N_B, LANES), F32)] * 2,
        compiler_params=_cparams("arbitrary"),
        name="rwkv_scan",
    )(r, v, kk, ws[0], kds[0], bs[0], r, v, kk, ws[1], kds[1], bs[1])


def _merge_kernel(o_ref, w_ref, g_ref, out_ref, acc_ref):
    j = pl.program_id(2)
    part = jax.nn.sigmoid(g_ref[...]) * jnp.dot(o_ref[0], w_ref[0], preferred_element_type=F32)

    @pl.when(j == 0)
    def _():
        acc_ref[...] = part

    @pl.when(j > 0)
    def _():
        acc_ref[...] += part

    @pl.when(j == N_BRANCH - 1)
    def _():
        out_ref[...] = acc_ref[...].astype(out_ref.dtype)


def _merge(o3, w_branch, p, gate_off):
    _, T, W = o3.shape
    D = w_branch.shape[2]
    tm = _tile(T, (640, 512, 256, 128))
    tn = _tile(D, (2048, 1024, 512, 256, 128))
    assert gate_off % tn == 0
    gb, nj = gate_off // tn, D // tn
    return pl.pallas_call(
        _merge_kernel,
        grid=(T // tm, D // tn, N_BRANCH),
        in_specs=[pl.BlockSpec((1, tm, W), lambda i, n, j: (j, i, 0)),
                  pl.BlockSpec((1, W, tn), lambda i, n, j: (j, 0, n)),
                  pl.BlockSpec((tm, tn), lambda i, n, j: (i, gb + j * nj + n))],
        out_specs=pl.BlockSpec((tm, tn), lambda i, n, j: (i, n)),
        out_shape=jax.ShapeDtypeStruct((T, D), BF16),
        scratch_shapes=[pltpu.VMEM((tm, tn), F32)],
        compiler_params=_cparams("parallel", "parallel", "arbitrary"),
        name="merge",
    )(o3, w_branch, p)


def _router_kernel(x_ref, w_ref, b_ref, e_ref, g_ref):
    logits = jnp.dot(x_ref[...], w_ref[...], preferred_element_type=F32,
                     precision=lax.Precision.HIGHEST) + b_ref[...]
    lane = lax.broadcasted_iota(jnp.int32, logits.shape, 1)
    vals, idxs = [], []
    x = logits
    for _ in range(TOP_K):
        m = jnp.max(x, axis=-1, keepdims=True)
        idx = jnp.min(jnp.where(x == m, lane, LANES), axis=-1, keepdims=True)
        vals.append(m)
        idxs.append(idx)
        x = jnp.where(lane == idx, -jnp.inf, x)
    ex = [jnp.exp(v - vals[0]) for v in vals]
    tot = ex[0] + ex[1] + ex[2] + ex[3]
    e_out = jnp.zeros(logits.shape, jnp.int32)
    g_out = jnp.zeros(logits.shape, F32)
    for k in range(TOP_K):
        e_out = jnp.where(lane == k, idxs[k], e_out)
        g_out = jnp.where(lane == k, ex[k] / tot, g_out)
    e_ref[...] = e_out
    g_ref[...] = g_out


def _router(h, router_w, router_b):
    T, D = h.shape
    tm = _tile(T, (256, 128))
    w = jnp.zeros((D, LANES), F32).at[:, :N_EXPERTS].set(router_w)
    b = jnp.full((1, LANES), -1e30, F32).at[0, :N_EXPERTS].set(router_b)
    return pl.pallas_call(
        _router_kernel,
        grid=(T // tm,),
        in_specs=[pl.BlockSpec((tm, D), lambda i: (i, 0)),
                  pl.BlockSpec((D, LANES), lambda i: (0, 0)),
                  pl.BlockSpec((1, LANES), lambda i: (0, 0))],
        out_specs=[pl.BlockSpec((tm, LANES), lambda i: (i, 0))] * 2,
        out_shape=[jax.ShapeDtypeStruct((T, LANES), jnp.int32), jax.ShapeDtypeStruct((T, LANES), F32)],
        compiler_params=_cparams("parallel"),
        name="router_topk",
    )(h, w, b)


def _row_gather(src_ref, dst_ref, sem, index_of, n):
    def copy(r):
        return pltpu.make_async_copy(src_ref.at[pl.ds(index_of(r), 1), :], dst_ref.at[pl.ds(r, 1), :], sem)

    def start():
        for r in range(n):
            copy(r).start()

    def wait():
        for r in range(n):
            copy(r).wait()

    return start, wait


def _expert_kernel(be_ref, nu_ref, idx_ref, nidx_ref, h_ref, wgu_ref, bgu_ref, wd_ref, bd_ref, o_ref, xbuf, sem):
    i = pl.program_id(0)
    n_used = nu_ref[0]
    slot = lax.rem(i, 2)

    def gather(ids_ref, s):
        return _row_gather(h_ref, xbuf.at[s], sem.at[s], lambda r: ids_ref[0, 0, r], MOE_BLOCK)

    @pl.when(i == 0)
    def _():
        gather(idx_ref, 0)[0]()

    @pl.when(i + 1 < n_used)
    def _():
        gather(nidx_ref, 1 - slot)[0]()

    @pl.when(i < n_used)
    def _():
        gather(idx_ref, slot)[1]()
        gu = jnp.dot(xbuf[slot].astype(BF16), wgu_ref[0], preferred_element_type=F32) + bgu_ref[0]
        gate = jnp.minimum(gu[:, :D_EXPERT], SWIGLU_LIMIT)
        up = jnp.clip(gu[:, D_EXPERT:], -SWIGLU_LIMIT, SWIGLU_LIMIT)
        act = (up + 1.0) * gate * jax.nn.sigmoid(gate * SWIGLU_ALPHA)
        o_ref[...] = jnp.dot(act.astype(BF16), wd_ref[0], preferred_element_type=F32) + bd_ref[0]

    @pl.when(i >= n_used)
    def _():
        o_ref[...] = jnp.zeros_like(o_ref)


def _experts(h, row_tok, block_e, n_used, w_gu, b_gu, w_down, b_down):
    D = h.shape[1]
    n_rows = row_tok.shape[0]
    nb = n_rows // MOE_BLOCK
    ids = row_tok.reshape(nb, 1, MOE_BLOCK)
    grid_spec = pltpu.PrefetchScalarGridSpec(
        num_scalar_prefetch=2,
        grid=(nb,),
        in_specs=[pl.BlockSpec((1, 1, MOE_BLOCK), lambda i, be, nu: (i, 0, 0), memory_space=pltpu.SMEM),
                  pl.BlockSpec((1, 1, MOE_BLOCK), lambda i, be, nu: (jnp.minimum(i + 1, nb - 1), 0, 0),
                               memory_space=pltpu.SMEM),
                  pl.BlockSpec(memory_space=pl.ANY),
                  pl.BlockSpec((1, D, 2 * D_EXPERT), lambda i, be, nu: (be[i], 0, 0)),
                  pl.BlockSpec((1, 1, 2 * D_EXPERT), lambda i, be, nu: (be[i], 0, 0)),
                  pl.BlockSpec((1, D_EXPERT, D), lambda i, be, nu: (be[i], 0, 0)),
                  pl.BlockSpec((1, 1, D), lambda i, be, nu: (be[i], 0, 0))],
        out_specs=pl.BlockSpec((MOE_BLOCK, D), lambda i, be, nu: (i, 0)),
        scratch_shapes=[pltpu.VMEM((2, MOE_BLOCK, D), F32), pltpu.SemaphoreType.DMA((2,))],
    )
    return pl.pallas_call(
        _expert_kernel,
        grid_spec=grid_spec,
        out_shape=jax.ShapeDtypeStruct((n_rows, D), F32),
        compiler_params=_cparams("arbitrary"),
        name="experts",
    )(block_e, n_used, ids, ids, h, w_gu, b_gu.reshape(N_EXPERTS, 1, -1), w_down,
      b_down.reshape(N_EXPERTS, 1, -1))


def _combine_kernel(idx_ref, nidx_ref, g_ref, res_ref, mod_ref, y_ref, o_ref, buf, sem, *, ctx_rows, tm, nt):
    i = pl.program_id(0)
    slot = lax.rem(i, 2)

    def gathers(ids_ref, s):
        return [_row_gather(y_ref, buf.at[s, k], sem.at[s], functools.partial(lambda r, k: ids_ref[0, k, r], k=k), tm)
                for k in range(TOP_K)]

    @pl.when(i == 0)
    def _():
        for start, _ in gathers(idx_ref, 0):
            start()

    @pl.when(i + 1 < nt)
    def _():
        for start, _ in gathers(nidx_ref, 1 - slot):
            start()

    for _, wait in gathers(idx_ref, slot):
        wait()
    g = g_ref[...]
    y = g[:, 0:1] * buf[slot, 0]
    for k in range(1, TOP_K):
        y = y + g[:, k:k + 1] * buf[slot, k]
    rows = i * tm + lax.broadcasted_iota(jnp.int32, (tm, 1), 0)
    m = jnp.where(rows < ctx_rows, mod_ref[1:2, :], mod_ref[0:1, :])
    o_ref[...] = res_ref[...] + m * y


def _combine(yb, dest, gates, resid, mod, ctx_rows):
    T, D = resid.shape
    tm = _tile(T, (128,))
    nt = T // tm
    ids = jnp.swapaxes(dest.reshape(nt, tm, TOP_K), 1, 2)
    return pl.pallas_call(
        functools.partial(_combine_kernel, ctx_rows=ctx_rows, tm=tm, nt=nt),
        grid=(nt,),
        in_specs=[pl.BlockSpec((1, TOP_K, tm), lambda i: (i, 0, 0), memory_space=pltpu.SMEM),
                  pl.BlockSpec((1, TOP_K, tm), lambda i: (jnp.minimum(i + 1, nt - 1), 0, 0),
                               memory_space=pltpu.SMEM),
                  pl.BlockSpec((tm, LANES), lambda i: (i, 0)),
                  pl.BlockSpec((tm, D), lambda i: (i, 0)),
                  pl.BlockSpec((8, D), lambda i: (0, 0)),
                  pl.BlockSpec(memory_space=pl.ANY)],
        out_specs=pl.BlockSpec((tm, D), lambda i: (i, 0)),
        out_shape=jax.ShapeDtypeStruct((T, D), F32),
        scratch_shapes=[pltpu.VMEM((2, TOP_K, tm, D), F32), pltpu.SemaphoreType.DMA((2,))],
        compiler_params=_cparams("arbitrary"),
        name="moe_combine",
    )(ids, ids, gates, resid, mod, yb)


def _moe(h2, resid, mod, ctx_rows, router_w, router_b, w_gu, b_gu, w_down, b_down):
    T, D = h2.shape
    e128, g128 = _router(h2, router_w, router_b)
    top_e = e128[:, :TOP_K]
    onehot = (top_e[:, :, None] == jnp.arange(N_EXPERTS, dtype=jnp.int32)).astype(jnp.int32).sum(1)
    counts = onehot.sum(0)
    rank = jnp.cumsum(onehot, axis=0) - onehot
    pcounts = (counts + MOE_BLOCK - 1) // MOE_BLOCK * MOE_BLOCK
    pends = jnp.cumsum(pcounts)
    pstarts = pends - pcounts
    dest = (pstarts[top_e] + jnp.take_along_axis(rank, top_e, axis=1)).astype(jnp.int32)
    n_blocks = -(-T * TOP_K // MOE_BLOCK) + N_EXPERTS
    n_rows = n_blocks * MOE_BLOCK
    flat_dest = dest.reshape(-1)
    tok = jnp.repeat(jnp.arange(T, dtype=jnp.int32), TOP_K)
    row_tok = jnp.zeros((n_rows,), jnp.int32).at[flat_dest].set(tok)
    block_e = jnp.minimum(
        jnp.searchsorted(pends, jnp.arange(n_blocks, dtype=pends.dtype) * MOE_BLOCK, side='right'),
        N_EXPERTS - 1).astype(jnp.int32)
    n_used = (pends[-1] // MOE_BLOCK).astype(jnp.int32).reshape(1)
    yb = _experts(h2, row_tok, block_e, n_used, w_gu, b_gu, w_down, b_down)
    return _combine(yb, dest, g128, resid, mod, ctx_rows)


def _rope_tables(n_ctx, S):
    half = MLA_ROPE // 2
    inv = ROPE_THETA ** (-jnp.arange(0, half, 2, dtype=F32) / half)
    rows = jnp.repeat(jnp.arange(S // GRID_W), GRID_W).astype(F32)[:, None] * inv
    cols = jnp.tile(jnp.arange(GRID_W), S // GRID_W).astype(F32)[:, None] * inv
    ang = jnp.concatenate([rows, rows, cols, cols], -1)
    cos = jnp.concatenate([jnp.ones((n_ctx, MLA_ROPE), F32), jnp.cos(ang)], 0)
    sin = jnp.concatenate([jnp.zeros((n_ctx, MLA_ROPE), F32), jnp.sin(ang)], 0)
    first = (jnp.arange(LANES) % (MLA_ROPE // 2)) < MLA_ROPE // 4
    cos2, sin2 = (jnp.concatenate([t, t], -1) * (SWA_SCALE * LOG2E) for t in (cos, sin))
    swa_tabs = (cos2, jnp.where(first, -sin2, 0.0), jnp.where(first, 0.0, sin2))
    return cos, sin, swa_tabs


def _rope(x, cos, sin):
    q = MLA_ROPE // 4

    def rot(p):
        return jnp.concatenate([-p[..., q:], p[..., :q]], -1)

    xr = jnp.concatenate([rot(x[..., :2 * q]), rot(x[..., 2 * q:])], -1)
    return x * cos[:, None, :] + xr * sin[:, None, :]


DECAY_GAIN = float(np.exp(-0.5))
LORA_W = 1024
O_WD_B, O_AD_F, O_AD_B, O_GD = R_W, 2 * R_W, 2 * R_W + R_A, 2 * R_W + 2 * R_A


def _seg_matrix():
    i = lax.broadcasted_iota(jnp.int32, (LANES, LANES), 0) // N_B
    j = lax.broadcasted_iota(jnp.int32, (LANES, LANES), 1) // N_B
    return (i == j).astype(BF16)


def _head_sums(x, seg):
    outs = []
    for c in range(x.shape[1] // LANES):
        rem = x[:, c * LANES:(c + 1) * LANES]
        tot = None
        for _ in range(3):
            hi = rem.astype(BF16)
            y = jnp.dot(hi, seg, preferred_element_type=F32)
            tot = y if tot is None else tot + y
            rem = rem - hi.astype(F32)
        outs.append(tot)
    return jnp.concatenate(outs, axis=1)


def _rwkv_prep_kernel(pr_ref, pk_ref, pv_ref, pl_ref, hp_r, hp_k, hp_v, hp_l, hn_r, hn_k, hn_v, hn_l,
                      mu_r, mu_k, mu_v, mu_l, kk_g, ka_g, w0_ref, a0_ref, wwf, wwb, waf, wab, wg,
                      r_o, v_o, kk_o, wf_o, kdf_o, bf_o, wb_o, kdb_o, bb_o, g_o, *, tm, n_ctx, n_tok):
    row = pl.program_id(0) * tm + lax.broadcasted_iota(jnp.int32, (tm, 1), 0)
    local = lax.broadcasted_iota(jnp.int32, (tm, 1), 0)
    has_prev = (row != 0) & (row != n_ctx)
    has_next = (row != n_ctx - 1) & (row != n_tok - 1)

    def shifted(x_ref, hp_ref, hn_ref, mu_ref):
        x = x_ref[...]
        prev = jnp.where(local == 0, hp_ref[7:8, :], pltpu.roll(x, 1, axis=0))
        nxt = jnp.where(local == tm - 1, hn_ref[0:1, :], pltpu.roll(x, tm - 1, axis=0))
        prev = jnp.where(has_prev, prev, 0.0)
        nxt = jnp.where(has_next, nxt, 0.0)
        return x + mu_ref[0:1, :] * (prev - x) + mu_ref[1:2, :] * (nxt - x)

    r = shifted(pr_ref, hp_r, hn_r, mu_r)
    k = shifted(pk_ref, hp_k, hn_k, mu_k)
    v = shifted(pv_ref, hp_v, hn_v, mu_v)
    lx = shifted(pl_ref, hp_l, hn_l, mu_l)
    seg = _seg_matrix()
    kk = k * kk_g[...]
    kk = kk * lax.rsqrt(_head_sums(kk * kk, seg) + 1e-12)
    r_o[...] = r
    v_o[...] = v
    kk_o[...] = kk
    th = jnp.tanh(lx[:, 0:2 * LANES]).astype(BF16)
    lin = lx[:, LANES:3 * LANES].astype(BF16)
    w_pre = (jnp.dot(th[:, 0:LANES], wwf[...], preferred_element_type=F32),
             jnp.dot(th, wwb[...], preferred_element_type=F32))
    a_pre = (jnp.dot(lin, waf[...], preferred_element_type=F32),
             jnp.dot(lin[:, LANES:], wab[...], preferred_element_type=F32))
    for d, (w_o, kd_o, b_o) in enumerate(((wf_o, kdf_o, bf_o), (wb_o, kdb_o, bb_o))):
        wp = w0_ref[d:d + 1, :] + w_pre[d]
        w_o[...] = jnp.exp(-DECAY_GAIN * jax.nn.sigmoid(wp))
        a = jax.nn.sigmoid(a0_ref[d:d + 1, :] + a_pre[d])
        kd_o[...] = k * (1.0 + (a - 1.0) * ka_g[...])
        b_o[...] = kk * a
    gd = jax.nn.sigmoid(lx[:, O_GD:O_GD + R_G]).astype(BF16)
    g_o[...] = jnp.dot(gd, wg[...], preferred_element_type=F32)


def _rows8(a):
    return jnp.pad(a, ((0, 8 - a.shape[0]), (0, 0))).astype(F32)


def _rwkv_prep(p, lo, lp, n_ctx):
    T = p.shape[0]
    tm = 128
    assert T % tm == 0 and n_ctx % tm == 0
    hb = tm // 8
    last8 = T // 8 - 1
    main = lambda w, off: pl.BlockSpec((tm, w), lambda i: (i, off // w))
    halo_p = lambda w, off: pl.BlockSpec((8, w), lambda i: (jnp.maximum(i * hb - 1, 0), off // w))
    halo_n = lambda w, off: pl.BlockSpec((8, w), lambda i: (jnp.minimum((i + 1) * hb, last8), off // w))
    groups = ((W_B, lo['r']), (W_B, lo['k']), (W_B, lo['v']), (LORA_W, lo['lora']))
    for w, off in groups:
        assert off % w == 0
    full = lambda a: pl.BlockSpec(a.shape, lambda i: (0, 0))
    mu = lp['rwkv_mu']
    mus = [_rows8(mu[:, 0:W_B]), _rows8(mu[:, W_B:2 * W_B]), _rows8(mu[:, 2 * W_B:3 * W_B]),
           _rows8(jnp.pad(mu[:, 3 * W_B:], ((0, 0), (0, LORA_W - (C_RWKV - 3 * W_B)))))]

    def slot(w, lo_row, n_rows):
        return jnp.pad(w, ((lo_row, n_rows - lo_row - w.shape[0]), (0, 0))).astype(BF16)

    params = mus + [lp['rwkv_k_k'].reshape(1, W_B), lp['rwkv_k_a'].reshape(1, W_B),
                    _rows8(lp['rwkv_w0']), _rows8(lp['rwkv_a0']),
                    slot(lp['rwkv_w_up'][0], 0, LANES), slot(lp['rwkv_w_up'][1], O_WD_B, 2 * LANES),
                    slot(lp['rwkv_a_up'][0], O_AD_F - LANES, 2 * LANES),
                    slot(lp['rwkv_a_up'][1], O_AD_B - 2 * LANES, LANES),
                    lp['rwkv_g_up'].astype(BF16)]
    in_specs = ([main(w, off) for w, off in groups] + [halo_p(w, off) for w, off in groups]
                + [halo_n(w, off) for w, off in groups] + [full(a) for a in params])
    outs = pl.pallas_call(
        functools.partial(_rwkv_prep_kernel, tm=tm, n_ctx=n_ctx, n_tok=T),
        grid=(T // tm,),
        in_specs=in_specs,
        out_specs=[pl.BlockSpec((tm, W_B), lambda i: (i, 0))] * 10,
        out_shape=[jax.ShapeDtypeStruct((T, W_B), F32)] * 10,
        compiler_params=_cparams("parallel"),
        name="rwkv_prep",
    )(*([p] * 12 + params))
    return outs


def _rwkv_out_kernel(of_ref, ob_ref, r_ref, v_ref, kdf_ref, kdb_ref, g_ref, lng_ref, lnb_ref, rk_ref, o_ref):
    seg = _seg_matrix()
    o = of_ref[...] + ob_ref[...]
    d = o - _head_sums(o, seg) * (1.0 / N_B)
    var = _head_sums(d * d, seg) * (1.0 / N_B)
    on = d * lax.rsqrt(var + GN_EPS) * lng_ref[...] + lnb_ref[...]
    coef = _head_sums(r_ref[...] * (kdf_ref[...] + kdb_ref[...]) * rk_ref[...], seg)
    o_ref[...] = ((on + coef * v_ref[...]) * g_ref[...]).astype(o_ref.dtype)


def _rwkv_out(o_f, o_b, r, v, kd_f, kd_b, g, lp):
    T = r.shape[0]
    tm = _tile(T, (128,))
    blk = pl.BlockSpec((tm, W_B), lambda i: (i, 0))
    par = pl.BlockSpec((1, W_B), lambda i: (0, 0))
    return pl.pallas_call(
        _rwkv_out_kernel,
        grid=(T // tm,),
        in_specs=[blk] * 7 + [par] * 3,
        out_specs=blk,
        out_shape=jax.ShapeDtypeStruct((T, W_B), BF16),
        compiler_params=_cparams("parallel"),
        name="rwkv_out",
    )(o_f, o_b, r, v, kd_f, kd_b, g, lp['rwkv_ln_g'].reshape(1, W_B), lp['rwkv_ln_b'].reshape(1, W_B),
      lp['rwkv_r_k'].reshape(1, W_B))


def _rwkv_mixer(p, lo, lp, n_ctx):
    r, v, kk, w_f, kd_f, b_f, w_b, kd_b, b_b, g = _rwkv_prep(p, lo, lp, n_ctx)
    o_f, o_b = _rwkv_scan(r, v, kk, (w_f, w_b), (kd_f, kd_b), (b_f, b_b), n_ctx=n_ctx)
    return _rwkv_out(o_f, o_b, r, v, kd_f, kd_b, g, lp)


def _in_layout(D):
    o = {}
    o['g'] = 0
    o['r'] = N_BRANCH * D
    o['k'] = o['r'] + W_B
    o['v'] = o['k'] + W_B
    o['sq'] = o['v'] + W_B
    o['q'] = o['sq'] + W_C
    o['lora'] = o['q'] + Q_LORA
    o['kv'] = o['lora'] + LORA_W
    o['sk'] = o['kv'] + KV_LORA
    o['sv'] = o['sk'] + KV_C * HD_C
    o['kr'] = o['sv'] + KV_C * HD_C
    o['end'] = -(-(o['kr'] + MLA_ROPE) // 1024) * 1024
    return o


def _prep_w_in(w_in, D):
    s = [int(v) for v in np.cumsum([0, Q_LORA, KV_LORA, MLA_ROPE, C_RWKV, W_C, KV_C * HD_C, KV_C * HD_C,
                                    N_BRANCH * D])]
    seg = lambda i: w_in[:, s[i]:s[i + 1]]
    rw = seg(3)
    zeros = lambda n: jnp.zeros((w_in.shape[0], n), w_in.dtype)
    lo = _in_layout(D)
    cols = [seg(7), rw[:, :3 * W_B], seg(4), seg(0), rw[:, 3 * W_B:], zeros(LORA_W - (C_RWKV - 3 * W_B)), seg(1),
            seg(5), seg(6), seg(2), zeros(lo['end'] - lo['kr'] - MLA_ROPE)]
    return jnp.concatenate(cols, -1).astype(BF16)


def _rot_cols(w):
    q = MLA_ROPE // 4
    return jnp.concatenate([-w[..., q:2 * q], w[..., :q], -w[..., 3 * q:], w[..., 2 * q:3 * q]], -1)


def _prep_wq_up(wq_up):
    w = wq_up.reshape(Q_LORA, H_A, MLA_NOPE + MLA_ROPE)
    w = jnp.concatenate([w, _rot_cols(w[..., MLA_NOPE:])], -1)
    return w.reshape(Q_LORA, H_A * MLA_QW).astype(BF16)


def _layer(xall, mod, lp, rope, n_ctx):
    T, D = xall.shape
    S = T - n_ctx
    cos, sin, swa_tabs = rope
    md = [mod[:, j * D:(j + 1) * D] for j in range(6)]
    lo = _in_layout(D)

    h1 = _rownorm(xall, lp['norm1_g'], shift=md[0], scale=md[1], ctx_rows=n_ctx, name="norm1")
    w_in = _prep_w_in(lp['w_in'], D)
    p = _matmul(h1, w_in, tm=_tile(T, (1280, 640, 256, 128)), tn=512, name="w_in")

    qn = _rownorm(p, lp['mla_qn_g'], col_off=lo['q'], width=Q_LORA, name="mla_qnorm")
    q_tab = jnp.concatenate([jnp.ones((T, MLA_NOPE), F32), cos, sin], -1) * (MLA_SCALE * LOG2E)
    q = _matmul(qn, _prep_wq_up(lp['mla_wq_up']), row_table=q_tab, out_dtype=BF16, name="mla_q_up")
    kvn = _rownorm(p, lp['mla_kvn_g'], col_off=lo['kv'], width=KV_LORA, name="mla_kvnorm")
    kv = _matmul(kvn, lp['mla_wkv_up'].astype(BF16), out_dtype=BF16, name="mla_kv_up")
    kr = _rope(p[:, None, lo['kr']:lo['kr'] + MLA_ROPE], cos, sin)[:, 0]
    kr = jnp.concatenate([kr, kr], -1).astype(BF16)
    oa_x = _mla_attention(q[n_ctx:], kv, kr, n_q=S, n_keys=T, name="mla_latent")
    oa_c = _mla_attention(q[:n_ctx], kv, kr, n_q=n_ctx, n_keys=n_ctx, name="mla_context")
    o_a = jnp.concatenate([oa_c, oa_x], 0)

    o_b = _rwkv_mixer(p, lo, lp, n_ctx)

    kvw = KV_C * HD_C
    kc = _rope(p[:, lo['sk']:lo['sk'] + kvw].reshape(T, KV_C, HD_C), cos, sin).reshape(T, kvw)
    vc = p[:, lo['sv']:lo['sv'] + kvw]
    sink_rows = jnp.repeat(lp['swa_sink'].astype(F32).reshape(KV_C, G_C) * LOG2E, 128, axis=1)
    sink_rows = jnp.broadcast_to(sink_rows.reshape(KV_C, G_C * 128, 1), (KV_C, G_C * 128, LANES))
    oc_x = _swa_attention(p, lo['sq'], swa_tabs, kc, vc, sink_rows, n_ctx=n_ctx, band=True, name="swa_latent")
    oc_c = _swa_attention(p, lo['sq'], swa_tabs, kc, vc, sink_rows, n_ctx=n_ctx, band=False, name="swa_context")
    o_c = jnp.concatenate([oc_c, oc_x], 0)

    m = _merge(jnp.stack([o_a, o_b, o_c], 0), lp['w_branch'].astype(BF16), p, lo['g'])
    x1 = _matmul(m, lp['w_out'].astype(BF16), resid=xall, mod=md[2], ctx_rows=n_ctx, name="w_out")
    h2 = _rownorm(x1, lp['norm2_g'], shift=md[3], scale=md[4], ctx_rows=n_ctx, out_dtype=F32, name="norm2")
    return _moe(h2, x1, md[5], n_ctx, lp['router_w'], lp['router_b'], lp['exp_w_gu'].astype(BF16),
                lp['exp_b_gu'], lp['exp_w_down'].astype(BF16), lp['exp_b_down'])


def kernel(x, c, ctx, c_ctx, ada_w, ada_b, norm1_g, w_in, mla_qn_g, mla_kvn_g, mla_wq_up, mla_wkv_up, rwkv_mu, rwkv_w0, rwkv_w_up, rwkv_a0, rwkv_a_up, rwkv_g_up, rwkv_k_k, rwkv_k_a, rwkv_r_k, rwkv_ln_g, rwkv_ln_b, swa_sink, w_branch, w_out, norm2_g, router_w, router_b, exp_w_gu, exp_b_gu, exp_w_down, exp_b_down, final_g):
    B, S, D = x.shape
    n_ctx = ctx.shape[1]
    assert B == 1 and c.shape[0] == 1
    depth = ada_w.shape[0]
    rope = _rope_tables(n_ctx, S)
    xall = jnp.concatenate([ctx[0], x[0]], 0)
    cond = jax.nn.silu(jnp.concatenate([c, c_ctx[None], jnp.zeros((6, D), F32)], 0))
    for i in range(depth):
        lp = dict(norm1_g=norm1_g[i], w_in=w_in[i],
                  mla_qn_g=mla_qn_g[i], mla_kvn_g=mla_kvn_g[i], mla_wq_up=mla_wq_up[i], mla_wkv_up=mla_wkv_up[i],
                  rwkv_mu=rwkv_mu[i], rwkv_w0=rwkv_w0[i], rwkv_w_up=rwkv_w_up[i], rwkv_a0=rwkv_a0[i],
                  rwkv_a_up=rwkv_a_up[i], rwkv_g_up=rwkv_g_up[i], rwkv_k_k=rwkv_k_k[i], rwkv_k_a=rwkv_k_a[i],
                  rwkv_r_k=rwkv_r_k[i], rwkv_ln_g=rwkv_ln_g[i], rwkv_ln_b=rwkv_ln_b[i], swa_sink=swa_sink[i],
                  w_branch=w_branch[i], w_out=w_out[i], norm2_g=norm2_g[i], router_w=router_w[i],
                  router_b=router_b[i], exp_w_gu=exp_w_gu[i], exp_b_gu=exp_b_gu[i],
                  exp_w_down=exp_w_down[i], exp_b_down=exp_b_down[i])
        mod = _matmul(cond, ada_w, b_slab=i, bias=ada_b[i], tm=8, tn=1024, name="adaln")
        xall = _layer(xall, mod, lp, rope, n_ctx)
    out = _rownorm(xall, final_g, row_off=n_ctx, rows=S, out_dtype=F32, name="final_norm")
    return out.reshape(B, S, D)
```
